```python
import math
import jax
import jax.numpy as jnp
from jax import lax
import numpy as np

D_MODEL = 2048
BATCH = 8
SEQ = 2048
DEPTH = 2
DEC_BATCH = 32
DEC_SEQ = 8
PAST_LEN = 8192
PAGE_SIZE = 128

HEAD_DIM = 64
NSA_HEADS = D_MODEL // (2 * HEAD_DIM)
MOBA_HEADS = D_MODEL // (2 * HEAD_DIM)
NSA_KV = 4
MOBA_KV = 4
NSA_REP = NSA_HEADS // NSA_KV
MOBA_REP = MOBA_HEADS // MOBA_KV
CMP_BLOCK = 64
CMP_HIDDEN = 256
SEL_BLOCK = CMP_BLOCK
N_SEL = 8
WINDOW = 512
WIN_QBLOCK = 128
WIN_CHUNK = 4
MOBA_BLOCK = 256
MOBA_TOPK = 3
MEM_HEADS = 4
MEM_HEAD_DIM = 128
MEM_LEN = 256
N_EXPERTS = 32
TOP_K = 4
D_FF = D_MODEL
SWIGLU_LIMIT = 7.0
SWIGLU_ALPHA = 1.702
MOE_BLOCK = 128
NSA_Q_CHUNK = 16
MOBA_Q_CHUNK = 4
LN_EPS = 1e-5
NEG_BIG = -1e30
DEEPNORM_ALPHA = (2 * DEPTH) ** 0.25
DEEPNORM_BETA = (8 * DEPTH) ** -0.25
SPLIT_SIZES = (NSA_HEADS * HEAD_DIM, NSA_KV * HEAD_DIM, NSA_KV * HEAD_DIM, NSA_KV * HEAD_DIM, NSA_KV * HEAD_DIM, NSA_KV * HEAD_DIM, NSA_KV * HEAD_DIM, 3 * NSA_HEADS, MOBA_HEADS * HEAD_DIM, MOBA_KV * HEAD_DIM, MOBA_KV * HEAD_DIM)
D_IN = (NSA_HEADS + MOBA_HEADS) * HEAD_DIM + (6 * NSA_KV + 2 * MOBA_KV) * HEAD_DIM + 3 * NSA_HEADS

kernel_name = 'nsa_moba_hybrid_decoder'


def alibi_slopes(n_heads, n_kv):
    slopes = 2.0 ** (-8.0 * np.arange(1, n_heads + 1) / n_heads)
    return jnp.asarray(slopes.reshape(n_kv, n_heads // n_kv), dtype=jnp.float32)


def layer_norm(x, g, b):
    xf = x.astype(jnp.float32)
    mu = jnp.mean(xf, axis=-1, keepdims=True)
    var = jnp.mean(jnp.square(xf - mu), axis=-1, keepdims=True)
    return ((xf - mu) * lax.rsqrt(var + LN_EPS)).astype(x.dtype) * g + b


def rms_norm(x, g):
    xf = x.astype(jnp.float32)
    return (xf * lax.rsqrt(jnp.mean(jnp.square(xf), axis=-1, keepdims=True) + LN_EPS)).astype(x.dtype) * g


def masked_softmax(s, mask, axis=-1):
    p = jax.nn.softmax(jnp.where(mask, s, NEG_BIG), axis=axis)
    return jnp.where(mask, p, 0.0)


def chunked_over_queries(fn, n_q, chunk, args, axes):
    c = math.gcd(chunk, n_q)
    nc = n_q // c

    def split(a, ax):
        a = jnp.moveaxis(a, ax, 0)
        a = a.reshape((nc, c) + a.shape[1:])
        return jnp.moveaxis(a, 1, ax + 1)

    out = lax.map(lambda xs: fn(*xs), tuple(split(a, ax) for a, ax in zip(args, axes)))
    out = jnp.moveaxis(out, 0, 1)
    return out.reshape((out.shape[0], n_q) + out.shape[3:])


def gather_pages(pool, layer, page_table):
    g = pool[layer, page_table]
    return g.reshape((g.shape[0], g.shape[1] * g.shape[2]) + g.shape[3:])


def nsa_compressed(q, qpos, k_all, v_all, pos_emb, ck_w1, ck_w2, cv_w1, cv_w2):
    b, length, g, dh = k_all.shape
    nb = length // CMP_BLOCK

    def compress(t, w1, w2):
        blk = t[:, : nb * CMP_BLOCK].reshape(b, nb, CMP_BLOCK, g, dh) + pos_emb[None, None, :, None, :]
        blk = blk.transpose(0, 1, 3, 2, 4).reshape(b, nb, g, CMP_BLOCK * dh)
        return jax.nn.silu(blk @ w1) @ w2

    k_cmp = compress(k_all, ck_w1, ck_w2)
    v_cmp = compress(v_all, cv_w1, cv_w2)
    k_end = jnp.arange(nb, dtype=jnp.int32) * CMP_BLOCK + (CMP_BLOCK - 1)
    dist = (qpos[:, None] - k_end[None, :]).astype(jnp.float32)
    s = jnp.einsum('btgrd,bngd->btgrn', q, k_cmp, preferred_element_type=jnp.float32) * (HEAD_DIM ** -0.5)
    s = s - alibi_slopes(NSA_HEADS, NSA_KV)[None, None, :, :, None] * jnp.abs(dist)[None, :, None, None, :]
    p = masked_softmax(s, (dist >= 0)[None, :, None, None, :])
    o = jnp.einsum('btgrn,bngd->btgrd', p.astype(v_cmp.dtype), v_cmp)
    return o, p.sum(axis=3)


def nsa_selected(q, qpos, k_all, v_all, imp):
    b, length, g, dh = k_all.shape
    nbs = -(-length // SEL_BLOCK)
    pad = nbs * SEL_BLOCK - length

    def blocks(t):
        t = jnp.pad(t, ((0, 0), (0, pad), (0, 0), (0, 0)))
        return t.reshape(b, nbs, SEL_BLOCK, g, dh).transpose(0, 3, 1, 2, 4)

    k_blk = blocks(k_all)
    v_blk = blocks(v_all)
    imp = jnp.pad(imp, ((0, 0), (0, 0), (0, 0), (0, nbs - imp.shape[-1])))
    blk = jnp.arange(nbs, dtype=jnp.int32)
    own = qpos // SEL_BLOCK
    visible = (blk[None, :] * SEL_BLOCK + SEL_BLOCK - 1) <= qpos[:, None]
    forced = (blk[None, :] == own[:, None]) | (blk[None, :] == 0)
    score = jnp.where(forced[None, :, None, :], jnp.inf, jnp.where(visible[None, :, None, :], imp, -jnp.inf))
    top_s, top_i = lax.top_k(score, min(N_SEL, nbs))
    slopes = alibi_slopes(NSA_HEADS, NSA_KV)
    offs = jnp.arange(SEL_BLOCK, dtype=jnp.int32)
    bi = jnp.arange(b)[:, None, None, None]
    gi = jnp.arange(g)[None, None, :, None]

    def chunk(q_c, pos_c, idx_c, sc_c):
        kg = k_blk[bi, gi, idx_c]
        vg = v_blk[bi, gi, idx_c]
        kpos = idx_c[..., None] * SEL_BLOCK + offs
        dist = (pos_c[None, :, None, None, None] - kpos).astype(jnp.float32)
        s = jnp.einsum('btgrd,btgksd->btgrks', q_c, kg, preferred_element_type=jnp.float32) * (HEAD_DIM ** -0.5)
        s = s - slopes[None, None, :, :, None, None] * jnp.abs(dist)[:, :, :, None]
        mask = ((sc_c > -jnp.inf)[..., None] & (dist >= 0))[:, :, :, None]
        p = masked_softmax(s, mask, axis=(-2, -1))
        return jnp.einsum('btgrks,btgksd->btgrd', p.astype(vg.dtype), vg)

    return chunked_over_queries(chunk, q.shape[1], NSA_Q_CHUNK, (q, qpos, top_i, top_s), (1, 0, 1, 1))


def window_attn(q, qpos, k, v, kpos):
    dist = qpos[:, :, None] - kpos[:, None, :]
    s = jnp.einsum('bntgrd,bnsgd->bntgrs', q, k, preferred_element_type=jnp.float32) * (HEAD_DIM ** -0.5)
    s = s - alibi_slopes(NSA_HEADS, NSA_KV)[None, None, None, :, :, None] * jnp.abs(dist.astype(jnp.float32))[None, :, :, None, None, :]
    mask = ((dist >= 0) & (dist < WINDOW) & (kpos[:, None, :] >= 0))[None, :, :, None, None, :]
    p = masked_softmax(s, mask)
    return jnp.einsum('bntgrs,bnsgd->bntgrd', p.astype(v.dtype), v)


def window_prompt(q, k, v):
    b, t = q.shape[:2]
    tb = math.gcd(WIN_QBLOCK, t)
    nb = t // tb
    idx = jnp.arange(nb, dtype=jnp.int32)[:, None] * tb + jnp.arange(WINDOW + tb, dtype=jnp.int32)[None, :]
    pad = ((0, 0), (WINDOW, 0), (0, 0), (0, 0))
    k_ctx = jnp.pad(k, pad)[:, idx]
    v_ctx = jnp.pad(v, pad)[:, idx]
    kpos = idx - WINDOW
    qpos = jnp.arange(t, dtype=jnp.int32).reshape(nb, tb)
    qb = q.reshape((b, nb, tb) + q.shape[2:])
    o = chunked_over_queries(window_attn, nb, WIN_CHUNK, (qb, qpos, k_ctx, v_ctx, kpos), (1, 0, 1, 1, 0))
    return o.reshape(q.shape)


def moba_attn(q, qpos, k_all, v_all):
    b, length, g, dh = k_all.shape
    nbm = -(-length // MOBA_BLOCK)
    pad = ((0, 0), (0, nbm * MOBA_BLOCK - length), (0, 0), (0, 0))
    k_blk = jnp.pad(k_all, pad).reshape(b, nbm, MOBA_BLOCK, g, dh)
    v_blk = jnp.pad(v_all, pad).reshape(b, nbm, MOBA_BLOCK, g, dh)
    k_mean = jnp.mean(k_blk.astype(jnp.float32), axis=2)
    gate = jnp.einsum('btgrd,bngd->btgrn', q.astype(jnp.float32), k_mean)
    own = qpos // MOBA_BLOCK
    blk = jnp.arange(nbm, dtype=jnp.int32)
    is_own = (blk[None, :] == own[:, None])[None, :, None, None, :]
    is_past = (blk[None, :] < own[:, None])[None, :, None, None, :]
    score = jnp.where(is_own, jnp.inf, jnp.where(is_past, gate, -jnp.inf))
    top_s, top_i = lax.top_k(score, min(MOBA_TOPK + 1, nbm))
    k_blk_t = k_blk.transpose(0, 3, 1, 2, 4)
    v_blk_t = v_blk.transpose(0, 3, 1, 2, 4)
    slopes = alibi_slopes(MOBA_HEADS, MOBA_KV)
    offs = jnp.arange(MOBA_BLOCK, dtype=jnp.int32)
    bi = jnp.arange(b)[:, None, None, None, None]
    gi = jnp.arange(g)[None, None, :, None, None]

    def chunk(q_c, pos_c, idx_c, sc_c):
        kg = k_blk_t[bi, gi, idx_c]
        vg = v_blk_t[bi, gi, idx_c]
        kpos = idx_c[..., None] * MOBA_BLOCK + offs
        dist = (pos_c[None, :, None, None, None, None] - kpos).astype(jnp.float32)
        s = jnp.einsum('btgrd,btgrksd->btgrks', q_c, kg, preferred_element_type=jnp.float32) * (HEAD_DIM ** -0.5)
        s = s - slopes[None, None, :, :, None, None] * jnp.abs(dist)
        mask = (sc_c > -jnp.inf)[..., None] & (dist >= 0)
        p = masked_softmax(s, mask, axis=(-2, -1))
        return jnp.einsum('btgrks,btgrksd->btgrd', p.astype(vg.dtype), vg)

    return chunked_over_queries(chunk, q.shape[1], MOBA_Q_CHUNK, (q, qpos, top_i, top_s), (1, 0, 1, 1))


def token_mixer(x, past, p):
    b, t, _ = x.shape
    proj = x @ p['w_in']
    split_points = np.cumsum(SPLIT_SIZES)[:-1].tolist()
    q_n, kc, vc, ks, vs, kw, vw, gt, q_m, km, vm = jnp.split(proj, split_points, axis=-1)
    q_n = q_n.reshape(b, t, NSA_KV, NSA_REP, HEAD_DIM)
    kc, vc, ks, vs, kw, vw = [a.reshape(b, t, NSA_KV, HEAD_DIM) for a in (kc, vc, ks, vs, kw, vw)]
    gt = gt.reshape(b, t, NSA_KV, NSA_REP, 3)
    q_m = q_m.reshape(b, t, MOBA_KV, MOBA_REP, HEAD_DIM)
    km = km.reshape(b, t, MOBA_KV, HEAD_DIM)
    vm = vm.reshape(b, t, MOBA_KV, HEAD_DIM)
    if past is None:
        offset = 0
        kc_all, vc_all, ks_all, vs_all, km_all, vm_all = kc, vc, ks, vs, km, vm
    else:
        offset = past['cmp_k'].shape[1]
        kc_all = jnp.concatenate([past['cmp_k'], kc], axis=1)
        vc_all = jnp.concatenate([past['cmp_v'], vc], axis=1)
        ks_all = jnp.concatenate([past['slc_k'], ks], axis=1)
        vs_all = jnp.concatenate([past['slc_v'], vs], axis=1)
        km_all = jnp.concatenate([past['moba_k'], km], axis=1)
        vm_all = jnp.concatenate([past['moba_v'], vm], axis=1)
    qpos = offset + jnp.arange(t, dtype=jnp.int32)
    o_cmp, imp = nsa_compressed(q_n, qpos, kc_all, vc_all, p['nsa_cmp_pos'], p['nsa_cmp_k_w1'], p['nsa_cmp_k_w2'], p['nsa_cmp_v_w1'], p['nsa_cmp_v_w2'])
    o_slc = nsa_selected(q_n, qpos, ks_all, vs_all, imp)
    if past is None:
        o_win = window_prompt(q_n, kw, vw)
        keep = min(WINDOW, t)
        win_k, win_v = kw[:, -keep:], vw[:, -keep:]
    else:
        wb = past['win_k'].shape[1]
        k_ctx = jnp.concatenate([past['win_k'], kw], axis=1)
        v_ctx = jnp.concatenate([past['win_v'], vw], axis=1)
        kpos = offset - wb + jnp.arange(wb + t, dtype=jnp.int32)
        o_win = window_attn(q_n[:, None], qpos[None], k_ctx[:, None], v_ctx[:, None], kpos[None])[:, 0]
        keep = min(WINDOW, wb + t)
        win_k, win_v = k_ctx[:, -keep:], v_ctx[:, -keep:]
    gates = jax.nn.sigmoid(gt.astype(jnp.float32)).astype(o_cmp.dtype)
    o_nsa = gates[..., 0:1] * o_cmp + gates[..., 1:2] * o_slc + gates[..., 2:3] * o_win
    o_moba = moba_attn(q_m, qpos, km_all, vm_all)
    h = jnp.concatenate([rms_norm(o_nsa.reshape(b, t, -1), p['gn_nsa']), rms_norm(o_moba.reshape(b, t, -1), p['gn_moba'])], axis=-1)
    return h @ p['w_out'], (kc, vc, ks, vs, km, vm, win_k, win_v)


def memory_attn(x, mem_k, mem_v, wq, wo):
    b, t, _ = x.shape
    q = (x @ wq).reshape(b, t, MEM_HEADS, MEM_HEAD_DIM)
    s = jnp.einsum('bthd,bmhd->bhtm', q, mem_k, preferred_element_type=jnp.float32) * (MEM_HEAD_DIM ** -0.5)
    p = jax.nn.softmax(s, axis=-1)
    o = jnp.einsum('bhtm,bmhd->bthd', p.astype(mem_v.dtype), mem_v)
    return o.reshape(b, t, MEM_HEADS * MEM_HEAD_DIM) @ wo


def moe_ffn(x, p, layer):
    n, d = x.shape
    logits = (x @ p['router_w']).astype(jnp.float32) + p['router_b'].astype(jnp.float32)
    top_v, top_e = lax.top_k(logits, TOP_K)
    gate = jax.nn.softmax(top_v, axis=-1)
    a = n * TOP_K
    e_flat = top_e.reshape(a)
    tok_flat = jnp.repeat(jnp.arange(n, dtype=jnp.int32), TOP_K)
    w_flat = gate.reshape(a)
    order = jnp.argsort(e_flat)
    e_s, tok_s, w_s = e_flat[order], tok_flat[order], w_flat[order]
    counts = jnp.zeros((N_EXPERTS,), jnp.int32).at[e_flat].add(1)
    padded = (counts + MOE_BLOCK - 1) // MOE_BLOCK * MOE_BLOCK
    pad_end = jnp.cumsum(padded)
    pad_start = pad_end - padded
    start = jnp.cumsum(counts) - counts
    dest = pad_start[e_s] + jnp.arange(a, dtype=jnp.int32) - start[e_s]
    n_blocks = -(-a // MOE_BLOCK) + N_EXPERTS
    cap = n_blocks * MOE_BLOCK
    src = jnp.full((cap,), n, jnp.int32).at[dest].set(tok_s)
    w_row = jnp.zeros((cap,), jnp.float32).at[dest].set(w_s)
    blk_start = jnp.arange(n_blocks, dtype=jnp.int32) * MOE_BLOCK
    blk_e = jnp.minimum(jnp.sum(blk_start[:, None] >= pad_end[None, :], axis=1), N_EXPERTS - 1).astype(jnp.int32)
    x_pad = jnp.concatenate([x, jnp.zeros((1, d), x.dtype)], axis=0)

    def expert_block(args):
        rows, e = args
        xb = x_pad[rows]
        g = xb @ p['exp_wg'][layer, e] + p['exp_bg'][layer, e]
        u = xb @ p['exp_wu'][layer, e] + p['exp_bu'][layer, e]
        g = jnp.minimum(g, SWIGLU_LIMIT)
        u = jnp.clip(u, -SWIGLU_LIMIT, SWIGLU_LIMIT)
        h = g * jax.nn.sigmoid(SWIGLU_ALPHA * g) * (u + 1.0)
        return h @ p['exp_wd'][layer, e] + p['exp_bd'][layer, e]

    y = lax.map(expert_block, (src.reshape(n_blocks, MOE_BLOCK), blk_e))
    y = y.reshape(cap, d) * w_row[:, None].astype(y.dtype)
    out = jnp.zeros((n + 1, d), y.dtype).at[src].add(y)
    return out[:n]


def decoder_layer(x, past, mem_k, mem_v, p, layer):
    y, mix_state = token_mixer(x, past, p)
    x = layer_norm(DEEPNORM_ALPHA * x + y, p['ln1_g'], p['ln1_b'])
    x = layer_norm(DEEPNORM_ALPHA * x + memory_attn(x, mem_k, mem_v, p['ca_wq'], p['ca_wo']), p['ln2_g'], p['ln2_b'])
    b, t, d = x.shape
    m = moe_ffn(x.reshape(b * t, d), p, layer).reshape(b, t, d)
    x = layer_norm(DEEPNORM_ALPHA * x + m, p['ln3_g'], p['ln3_b'])
    return x, mix_state


def setup_inputs(seed: int = 0) -> dict:
    key = jax.random.key(seed)
    keys = list(jax.random.split(key, 48))

    def normal(shape, scale=1.0):
        return jax.random.normal(keys.pop(), shape, jnp.float32) * scale

    def gain(shape):
        return 1.0 + normal(shape, 0.02)

    n_pages = PAST_LEN // PAGE_SIZE
    n_pool = (DEC_BATCH * n_pages * 5) // 4
    wbuf = min(WINDOW, PAST_LEN)
    nsa_pool = (DEPTH, n_pool, PAGE_SIZE, NSA_KV, HEAD_DIM)
    moba_pool = (DEPTH, n_pool, PAGE_SIZE, MOBA_KV, HEAD_DIM)
    beta = DEEPNORM_BETA
    seg_scale = (1.0, 1.0, beta, 1.0, beta, 1.0, beta, 1.0, 1.0, 1.0, beta)
    col_scale = np.concatenate([np.full(n, s, np.float32) for n, s in zip(SPLIT_SIZES, seg_scale)])
    x_prompt = normal((BATCH, SEQ, D_MODEL))
    x_sample = normal((DEC_BATCH, DEC_SEQ, D_MODEL))
    cache_nsa_cmp_k = normal(nsa_pool)
    cache_nsa_cmp_v = normal(nsa_pool)
    cache_nsa_slc_k = normal(nsa_pool)
    cache_nsa_slc_v = normal(nsa_pool)
    cache_moba_k = normal(moba_pool)
    cache_moba_v = normal(moba_pool)
    cache_nsa_win_k = normal((DEPTH, DEC_BATCH, wbuf, NSA_KV, HEAD_DIM))
    cache_nsa_win_v = normal((DEPTH, DEC_BATCH, wbuf, NSA_KV, HEAD_DIM))
    cache_mem_k = normal((DEPTH, DEC_BATCH, MEM_LEN, MEM_HEADS, MEM_HEAD_DIM))
    cache_mem_v = normal((DEPTH, DEC_BATCH, MEM_LEN, MEM_HEADS, MEM_HEAD_DIM))
    page_table = jax.random.permutation(keys.pop(), n_pool)[: DEC_BATCH * n_pages].reshape(DEC_BATCH, n_pages).astype(jnp.int32)
    mem_prompt = normal((BATCH, MEM_LEN, D_MODEL))
    d_mix = (NSA_HEADS + MOBA_HEADS) * HEAD_DIM
    d_mem = MEM_HEADS * MEM_HEAD_DIM
    return {
        'x_prompt': x_prompt, 'x_sample': x_sample,
        'cache_nsa_cmp_k': cache_nsa_cmp_k, 'cache_nsa_cmp_v': cache_nsa_cmp_v,
        'cache_nsa_slc_k': cache_nsa_slc_k, 'cache_nsa_slc_v': cache_nsa_slc_v,
        'cache_moba_k': cache_moba_k, 'cache_moba_v': cache_moba_v,
        'cache_nsa_win_k': cache_nsa_win_k, 'cache_nsa_win_v': cache_nsa_win_v,
        'cache_mem_k': cache_mem_k, 'cache_mem_v': cache_mem_v,
        'page_table': page_table, 'mem_prompt': mem_prompt,
        'w_in': normal((DEPTH, D_MODEL, D_IN), D_MODEL ** -0.5) * jnp.asarray(col_scale),
        'nsa_cmp_pos': normal((DEPTH, CMP_BLOCK, HEAD_DIM), 0.1),
        'nsa_cmp_k_w1': normal((DEPTH, CMP_BLOCK * HEAD_DIM, CMP_HIDDEN), (CMP_BLOCK * HEAD_DIM) ** -0.5),
        'nsa_cmp_k_w2': normal((DEPTH, CMP_HIDDEN, HEAD_DIM), CMP_HIDDEN ** -0.5),
        'nsa_cmp_v_w1': normal((DEPTH, CMP_BLOCK * HEAD_DIM, CMP_HIDDEN), (CMP_BLOCK * HEAD_DIM) ** -0.5),
        'nsa_cmp_v_w2': normal((DEPTH, CMP_HIDDEN, HEAD_DIM), CMP_HIDDEN ** -0.5),
        'gn_nsa': gain((DEPTH, NSA_HEADS * HEAD_DIM)),
        'gn_moba': gain((DEPTH, MOBA_HEADS * HEAD_DIM)),
        'w_out': normal((DEPTH, d_mix, D_MODEL), d_mix ** -0.5 * beta),
        'ln1_g': gain((DEPTH, D_MODEL)), 'ln1_b': normal((DEPTH, D_MODEL), 0.02),
        'ca_wq': normal((DEPTH, D_MODEL, d_mem), D_MODEL ** -0.5),
        'ca_wk': normal((DEPTH, D_MODEL, d_mem), D_MODEL ** -0.5),
        'ca_wv': normal((DEPTH, D_MODEL, d_mem), D_MODEL ** -0.5 * beta),
        'ca_wo': normal((DEPTH, d_mem, D_MODEL), d_mem ** -0.5 * beta),
        'ln2_g': gain((DEPTH, D_MODEL)), 'ln2_b': normal((DEPTH, D_MODEL), 0.02),
        'router_w': normal((DEPTH, D_MODEL, N_EXPERTS), D_MODEL ** -0.5),
        'router_b': normal((DEPTH, N_EXPERTS), 0.01),
        'exp_wg': normal((DEPTH, N_EXPERTS, D_MODEL, D_FF), D_MODEL ** -0.5),
        'exp_bg': normal((DEPTH, N_EXPERTS, D_FF), 0.02),
        'exp_wu': normal((DEPTH, N_EXPERTS, D_MODEL, D_FF), D_MODEL ** -0.5),
        'exp_bu': normal((DEPTH, N_EXPERTS, D_FF), 0.02),
        'exp_wd': normal((DEPTH, N_EXPERTS, D_FF, D_MODEL), D_FF ** -0.5 * beta),
        'exp_bd': normal((DEPTH, N_EXPERTS, D_MODEL), 0.02),
        'ln3_g': gain((DEPTH, D_MODEL)), 'ln3_b': normal((DEPTH, D_MODEL), 0.02),
    }


def reference(x_prompt, x_sample, cache_nsa_cmp_k, cache_nsa_cmp_v, cache_nsa_slc_k, cache_nsa_slc_v, cache_moba_k, cache_moba_v, cache_nsa_win_k, cache_nsa_win_v, cache_mem_k, cache_mem_v, page_table, mem_prompt, w_in, nsa_cmp_pos, nsa_cmp_k_w1, nsa_cmp_k_w2, nsa_cmp_v_w1, nsa_cmp_v_w2, gn_nsa, gn_moba, w_out, ln1_g, ln1_b, ca_wq, ca_wk, ca_wv, ca_wo, ln2_g, ln2_b, router_w, router_b, exp_wg, exp_bg, exp_wu, exp_bu, exp_wd, exp_bd, ln3_g, ln3_b):
    bp, n_mem = mem_prompt.shape[0], mem_prompt.shape[1]
    xp, xs = x_prompt, x_sample
    prompt_new = [[] for _ in range(10)]
    sample_new = [[] for _ in range(8)]
    for l in range(DEPTH):
        p = {
            'w_in': w_in[l], 'nsa_cmp_pos': nsa_cmp_pos[l],
            'nsa_cmp_k_w1': nsa_cmp_k_w1[l], 'nsa_cmp_k_w2': nsa_cmp_k_w2[l],
            'nsa_cmp_v_w1': nsa_cmp_v_w1[l], 'nsa_cmp_v_w2': nsa_cmp_v_w2[l],
            'gn_nsa': gn_nsa[l], 'gn_moba': gn_moba[l], 'w_out': w_out[l],
            'ln1_g': ln1_g[l], 'ln1_b': ln1_b[l],
            'ca_wq': ca_wq[l], 'ca_wo': ca_wo[l],
            'ln2_g': ln2_g[l], 'ln2_b': ln2_b[l],
            'router_w': router_w[l], 'router_b': router_b[l],
            'exp_wg': exp_wg, 'exp_bg': exp_bg, 'exp_wu': exp_wu, 'exp_bu': exp_bu,
            'exp_wd': exp_wd, 'exp_bd': exp_bd,
            'ln3_g': ln3_g[l], 'ln3_b': ln3_b[l],
        }
        mem_k = (mem_prompt @ ca_wk[l]).reshape(bp, n_mem, MEM_HEADS, MEM_HEAD_DIM)
        mem_v = (mem_prompt @ ca_wv[l]).reshape(bp, n_mem, MEM_HEADS, MEM_HEAD_DIM)
        xp, st = decoder_layer(xp, None, mem_k, mem_v, p, l)
        for acc, a in zip(prompt_new, st + (mem_k, mem_v)):
            acc.append(a)
        past = {
            'cmp_k': gather_pages(cache_nsa_cmp_k, l, page_table), 'cmp_v': gather_pages(cache_nsa_cmp_v, l, page_table),
            'slc_k': gather_pages(cache_nsa_slc_k, l, page_table), 'slc_v': gather_pages(cache_nsa_slc_v, l, page_table),
            'moba_k': gather_pages(cache_moba_k, l, page_table), 'moba_v': gather_pages(cache_moba_v, l, page_table),
            'win_k': cache_nsa_win_k[l], 'win_v': cache_nsa_win_v[l],
        }
        xs, st = decoder_layer(xs, past, cache_mem_k[l], cache_mem_v[l], p, l)
        for acc, a in zip(sample_new, st):
            acc.append(a)
    p_cmp_k, p_cmp_v, p_slc_k, p_slc_v, p_moba_k, p_moba_v, p_win_k, p_win_v, p_mem_k, p_mem_v = [jnp.stack(a) for a in prompt_new]
    s_cmp_k, s_cmp_v, s_slc_k, s_slc_v, s_moba_k, s_moba_v, s_win_k, s_win_v = [jnp.stack(a) for a in sample_new]
    y_prompt, y_sample = xp, xs
    return (y_prompt, y_sample, p_cmp_k, p_cmp_v, p_slc_k, p_slc_v, p_moba_k, p_moba_v, p_win_k, p_win_v, p_mem_k, p_mem_v, s_cmp_k, s_cmp_v, s_slc_k, s_slc_v, s_moba_k, s_moba_v, s_win_k, s_win_v)
```

```python
import functools

import numpy as np
import jax
import jax.numpy as jnp
from jax import lax
from jax.experimental import pallas as pl
from jax.experimental.pallas import tpu as pltpu

F32 = jnp.float32
BF16 = jnp.bfloat16

D_MODEL = 2048
HEAD_DIM = 64
N_KV = 4
N_REP = 4
N_HEADS = N_KV * N_REP
CMP_BLOCK = 64
SEL_BLOCK = 64
N_SEL = 8
WINDOW = 512
MOBA_BLOCK = 256
MOBA_TOPK = 3
MEM_HEADS = 4
MEM_HEAD_DIM = 128
N_EXPERTS = 32
TOP_K = 4
SWIGLU_LIMIT = 7.0
SWIGLU_ALPHA = 1.702
LN_EPS = 1e-5
NEG_BIG = -1e30
DEPTH = 2
DEEPNORM_ALPHA = (2 * DEPTH) ** 0.25
PAGE_SIZE = 128

KV_W = N_KV * HEAD_DIM
Q_W = N_HEADS * HEAD_DIM
TQ = 256
VMEM_LIMIT = 56 * 1024 * 1024
ROW_TILE = 256
MOE_TILE = 512
FF_CHUNK = 512


def _cparams(*sem):
    return pltpu.CompilerParams(dimension_semantics=sem, vmem_limit_bytes=VMEM_LIMIT)


def _dot(a, b):
    return jnp.dot(a, b, preferred_element_type=F32)


def _dot_t(a, b):
    return lax.dot_general(a, b, (((1,), (1,)), ((), ())), preferred_element_type=F32)


def _div_pow2(x, n):
    assert n & (n - 1) == 0
    return lax.shift_right_logical(x, n.bit_length() - 1)


def _layer_norm(z, g, b):
    mu = jnp.mean(z, axis=-1, keepdims=True)
    zc = z - mu
    var = jnp.mean(zc * zc, axis=-1, keepdims=True)
    return zc * lax.rsqrt(var + LN_EPS) * g + b


def _rms_norm(x, g):
    return x * lax.rsqrt(jnp.mean(x * x, axis=-1, keepdims=True) + LN_EPS) * g


def _masked_softmax(s, mask):
    s = jnp.where(mask, s, NEG_BIG)
    m = jnp.max(s, axis=-1, keepdims=True)
    e = jnp.where(mask, jnp.exp(s - m), 0.0)
    den = jnp.sum(e, axis=-1, keepdims=True)
    return e / jnp.where(den > 0.0, den, 1.0)


def _topk_bias(score, k, lane):
    lane = lane.astype(F32)
    cur = score
    bias = jnp.full(score.shape, NEG_BIG, F32)
    for _ in range(k):
        m = jnp.max(cur, axis=-1, keepdims=True)
        idx = jnp.min(jnp.where(cur == m, lane, float(score.shape[-1])), axis=-1, keepdims=True)
        pick = lane == idx
        bias = jnp.where(pick & (m > -jnp.inf), 0.0, bias)
        cur = jnp.where(pick, -jnp.inf, cur)
    return bias


def _flash_update(carry, q, k_t, v_t, bias):
    m, l, acc = carry
    s = _dot_t(q, k_t) + bias
    m_new = jnp.maximum(m, jnp.max(s, axis=-1, keepdims=True))
    alpha = jnp.exp(m - m_new)
    p = jnp.exp(s - m_new)
    l = alpha * l + jnp.sum(p, axis=-1, keepdims=True)
    acc = alpha * acc + _dot(p.astype(BF16), v_t)
    return m_new, l, acc


def _mm_kernel(x_ref, w_ref, o_ref):
    o_ref[...] = _dot(x_ref[...].astype(BF16), w_ref[...])


def matmul(x, w, tm, tn):
    m, k = x.shape
    n = w.shape[1]
    tm = max(t for t in range(8, min(tm, m) + 1, 8) if m % t == 0)
    assert n % tn == 0
    return pl.pallas_call(
        _mm_kernel,
        grid=(m // tm, n // tn),
        in_specs=[pl.BlockSpec((tm, k), lambda i, j: (i, 0)),
                  pl.BlockSpec((k, tn), lambda i, j: (0, j))],
        out_specs=pl.BlockSpec((tm, tn), lambda i, j: (i, j)),
        out_shape=jax.ShapeDtypeStruct((m, n), F32),
        compiler_params=_cparams("parallel", "arbitrary"),
        name="matmul",
    )(x, w)


def _cmp_mlp_kernel(x_ref, pos_ref, w1_ref, w2_ref, o_ref):
    x = (x_ref[...] + pos_ref[...]).astype(BF16)
    h = _dot(x, w1_ref[...])
    h = h * jax.nn.sigmoid(h)
    o_ref[...] = _dot(h.astype(BF16), w2_ref[...])


def cmp_mlp(x, pos, w1, w2, tr):
    r, k = x.shape
    hid = w1.shape[1]
    tr = min(tr, r)
    assert r % tr == 0
    return pl.pallas_call(
        _cmp_mlp_kernel,
        grid=(r // tr,),
        in_specs=[pl.BlockSpec((tr, k), lambda i: (i, 0)),
                  pl.BlockSpec((1, k), lambda i: (0, 0)),
                  pl.BlockSpec((k, hid), lambda i: (0, 0)),
                  pl.BlockSpec((hid, HEAD_DIM), lambda i: (0, 0))],
        out_specs=pl.BlockSpec((tr, HEAD_DIM), lambda i: (i, 0)),
        out_shape=jax.ShapeDtypeStruct((r, HEAD_DIM), F32),
        compiler_params=_cparams("parallel"),
        name="cmp_mlp",
    )(x, pos, w1, w2)


def _attn_prompt_kernel(slopes_ref, qn_ref, qm_ref, kc_ref, vc_ref, ks_ref, vs_ref, kw_ref, vw_ref,
                        km_ref, vm_ref, gt_ref, on_ref, om_ref, bias_ref, *, seq):
    g = pl.program_id(1)
    qt = pl.program_id(2)
    n_cmp = seq // CMP_BLOCK
    n_tiles = seq // TQ
    blocks_per_tile = TQ // SEL_BLOCK
    t0 = qt * TQ
    tq = t0 + lax.broadcasted_iota(jnp.int32, (TQ, 1), 0)
    key_rel = lax.broadcasted_iota(jnp.int32, (1, TQ), 1)
    key_rel_f = key_rel.astype(F32)
    causal = (t0 + key_rel) <= tq
    gates = jax.nn.sigmoid(gt_ref[0, 0])
    init = (jnp.full((TQ, 1), NEG_BIG, F32), jnp.zeros((TQ, 1), F32), jnp.zeros((TQ, HEAD_DIM), F32))

    kc = kc_ref[0, 0].astype(BF16)
    vc = vc_ref[0, 0].astype(BF16)
    blk = lax.broadcasted_iota(jnp.int32, (TQ, n_cmp), 1)
    dist_c = tq - (blk * CMP_BLOCK + (CMP_BLOCK - 1))
    vis_c = dist_c >= 0
    dist_cf = dist_c.astype(F32)
    imp = jnp.zeros((TQ, n_cmp), F32)
    o_cmp = []
    for r in range(N_REP):
        slope = slopes_ref[g * N_REP + r]
        p = _masked_softmax(_dot_t(qn_ref[0, 0, r], kc) - slope * dist_cf, vis_c)
        imp = imp + p
        o_cmp.append(_dot(p.astype(BF16), vc))

    forced = (blk == _div_pow2(tq, SEL_BLOCK)) | (blk == 0)
    score = jnp.where(forced, jnp.inf, jnp.where(vis_c, imp, -jnp.inf))
    sel_bias = _topk_bias(score, N_SEL, blk).astype(BF16)
    expand_n = lax.broadcasted_iota(jnp.int32, (n_cmp, TQ), 0)
    expand_j = _div_pow2(lax.broadcasted_iota(jnp.int32, (n_cmp, TQ), 1), SEL_BLOCK)
    for kt in range(n_tiles):
        expand = (expand_n == kt * blocks_per_tile + expand_j).astype(BF16)
        bias_ref[kt] = _dot(sel_bias, expand)

    km_all = km_ref[0, 0].astype(F32)
    k_mean = jnp.mean(km_all.reshape(seq // MOBA_BLOCK, MOBA_BLOCK, HEAD_DIM), axis=1).astype(BF16)
    mblk = lax.broadcasted_iota(jnp.int32, (TQ, seq // MOBA_BLOCK), 1)

    w_start = pl.multiple_of(jnp.maximum(qt - 2, 0) * TQ, TQ)
    w_len = WINDOW + TQ
    kw = kw_ref[0, 0, pl.ds(w_start, w_len), :]
    vw = vw_ref[0, 0, pl.ds(w_start, w_len), :]
    wpos = w_start + lax.broadcasted_iota(jnp.int32, (1, w_len), 1)
    w_valid = (wpos <= tq) & (wpos > tq - WINDOW)
    wpos_rel = (wpos - t0).astype(F32)

    for r in range(N_REP):
        slope = slopes_ref[g * N_REP + r]
        qn = qn_ref[0, 0, r]
        qm = qm_ref[0, 0, r]

        def slc_body(kt, carry, qn=qn, slope=slope):
            start = pl.multiple_of(kt * TQ, TQ)
            bias = bias_ref[kt] + slope * ((start - t0).astype(F32) + key_rel_f)
            return _flash_update(carry, qn, ks_ref[0, 0, pl.ds(start, TQ), :], vs_ref[0, 0, pl.ds(start, TQ), :], bias)

        carry = lax.fori_loop(0, qt, slc_body, init)
        d_start = pl.multiple_of(t0, TQ)
        bias = jnp.where(causal, bias_ref[qt] + slope * key_rel_f, NEG_BIG)
        _, l, acc = _flash_update(carry, qn, ks_ref[0, 0, pl.ds(d_start, TQ), :],
                                  vs_ref[0, 0, pl.ds(d_start, TQ), :], bias)
        o_slc = acc / l

        s = jnp.where(w_valid, _dot_t(qn, kw) + slope * wpos_rel, NEG_BIG)
        pw = jnp.exp(s - jnp.max(s, axis=-1, keepdims=True))
        o_win = _dot(pw.astype(BF16), vw) / jnp.sum(pw, axis=-1, keepdims=True)

        o_nsa = (gates[:, 3 * r:3 * r + 1] * o_cmp[r] + gates[:, 3 * r + 1:3 * r + 2] * o_slc
                 + gates[:, 3 * r + 2:3 * r + 3] * o_win)
        on_ref[0, :, r * HEAD_DIM:(r + 1) * HEAD_DIM] = o_nsa

        gate = _dot_t(qm, k_mean)
        m_bias = _topk_bias(jnp.where(mblk < qt, gate, -jnp.inf), MOBA_TOPK, mblk)

        def moba_body(kt, carry, qm=qm, slope=slope, m_bias=m_bias):
            start = pl.multiple_of(kt * TQ, TQ)
            col = jnp.sum(jnp.where(mblk == kt, m_bias, 0.0), axis=-1, keepdims=True)
            bias = col + slope * ((start - t0).astype(F32) + key_rel_f)
            return _flash_update(carry, qm, km_ref[0, 0, pl.ds(start, TQ), :], vm_ref[0, 0, pl.ds(start, TQ), :], bias)

        carry = lax.fori_loop(0, qt, moba_body, init)
        bias = jnp.where(causal, slope * key_rel_f, NEG_BIG)
        _, l, acc = _flash_update(carry, qm, km_ref[0, 0, pl.ds(d_start, TQ), :],
                                  vm_ref[0, 0, pl.ds(d_start, TQ), :], bias)
        om_ref[0, :, r * HEAD_DIM:(r + 1) * HEAD_DIM] = acc / l


def attn_prompt(slopes, qn, qm, kc, vc, ks, vs, kw, vw, km, vm, gt):
    b, _, _, seq, _ = qn.shape
    assert MOBA_BLOCK == TQ and seq % TQ == 0 and seq >= WINDOW + TQ
    n_cmp = seq // CMP_BLOCK
    q_spec = pl.BlockSpec((1, 1, N_REP, TQ, HEAD_DIM), lambda bi, g, qt: (bi, g, 0, qt, 0))
    c_spec = pl.BlockSpec((1, 1, n_cmp, HEAD_DIM), lambda bi, g, qt: (bi, g, 0, 0))
    kv_spec = pl.BlockSpec((1, 1, seq, HEAD_DIM), lambda bi, g, qt: (bi, g, 0, 0))
    o_spec = pl.BlockSpec((1, TQ, N_REP * HEAD_DIM), lambda bi, g, qt: (bi, qt, g))
    o_shape = jax.ShapeDtypeStruct((b, seq, Q_W), F32)
    return pl.pallas_call(
        functools.partial(_attn_prompt_kernel, seq=seq),
        grid=(b, N_KV, seq // TQ),
        in_specs=[pl.BlockSpec(memory_space=pltpu.SMEM), q_spec, q_spec, c_spec, c_spec,
                  kv_spec, kv_spec, kv_spec, kv_spec, kv_spec, kv_spec,
                  pl.BlockSpec((1, 1, TQ, 3 * N_REP), lambda bi, g, qt: (bi, g, qt, 0))],
        out_specs=[o_spec, o_spec],
        out_shape=[o_shape, o_shape],
        scratch_shapes=[pltpu.VMEM((seq // TQ, TQ, TQ), F32)],
        compiler_params=_cparams("parallel", "parallel", "arbitrary"),
        name="attn_prompt",
    )(slopes, qn, qm, kc, vc, ks, vs, kw, vw, km, vm, gt)


def _diag_blocks(o, rows_per_group):
    return jnp.concatenate(
        [o[g * rows_per_group:(g + 1) * rows_per_group, g * HEAD_DIM:(g + 1) * HEAD_DIM] for g in range(N_KV)], axis=0)


def _page_sum_kernel(pt_ref, *refs, pages_per_step):
    del pt_ref
    o_ref = refs[pages_per_step]
    for i in range(pages_per_step):
        o_ref[0, i:i + 1, :] = jnp.sum(refs[i][0], axis=0, keepdims=True)


def page_sums(pool, layer, page_table, pages_per_step):
    b, n_pages = page_table.shape
    n_pool = pool.shape[0] // DEPTH
    assert n_pages % pages_per_step == 0

    def page_spec(i):
        return pl.BlockSpec((1, PAGE_SIZE, KV_W),
                            lambda bi, s, pt: (layer * n_pool + pt[bi * n_pages + s * pages_per_step + i], 0, 0))

    return pl.pallas_call(
        functools.partial(_page_sum_kernel, pages_per_step=pages_per_step),
        grid_spec=pltpu.PrefetchScalarGridSpec(
            num_scalar_prefetch=1,
            grid=(b, n_pages // pages_per_step),
            in_specs=[page_spec(i) for i in range(pages_per_step)],
            out_specs=pl.BlockSpec((1, pages_per_step, KV_W), lambda bi, s, pt: (bi, s, 0)),
        ),
        out_shape=jax.ShapeDtypeStruct((b, n_pages, KV_W), F32),
        compiler_params=_cparams("parallel", "arbitrary"),
        name="page_sums",
    )(page_table.reshape(-1), *([pool] * pages_per_step))


def _sample_select_kernel(qn_ref, qm_ref, kc_ref, vc_ref, kmean_ref, slope_ref, tpos_ref,
                          ocmp_ref, sbias_ref, mbias_ref, *, past, n_new):
    rows = N_HEADS * n_new
    rpg = N_REP * n_new
    n_cmp = kc_ref.shape[1]
    n_sel_lanes = sbias_ref.shape[2]
    n_moba_lanes = mbias_ref.shape[2]
    slope = slope_ref[...]
    qpos = past + tpos_ref[...]

    blk = lax.broadcasted_iota(jnp.int32, (rows, n_cmp), 1)
    dist = qpos - (blk * CMP_BLOCK + (CMP_BLOCK - 1))
    p = _masked_softmax(_dot_t(qn_ref[0], kc_ref[0].astype(BF16)) - slope * dist.astype(F32), dist >= 0)
    ocmp_ref[0] = _diag_blocks(_dot(p.astype(BF16), vc_ref[0].astype(BF16)), rpg)

    imp = jnp.sum(p.reshape(N_KV, N_REP, n_new, n_cmp), axis=1).reshape(N_KV * n_new, n_cmp)
    imp = jnp.concatenate([imp, jnp.zeros((N_KV * n_new, n_sel_lanes - n_cmp), F32)], axis=1)
    sblk = lax.broadcasted_iota(jnp.int32, (N_KV * n_new, n_sel_lanes), 1)
    assert n_new & (n_new - 1) == 0
    spos = past + (lax.broadcasted_iota(jnp.int32, (N_KV * n_new, 1), 0) & (n_new - 1))
    n_sel_blocks = -(-(past + n_new) // SEL_BLOCK)
    visible = (sblk * SEL_BLOCK + (SEL_BLOCK - 1)) <= spos
    forced = (sblk == _div_pow2(spos, SEL_BLOCK)) | (sblk == 0)
    score = jnp.where(forced, jnp.inf, jnp.where(visible, imp, -jnp.inf))
    score = jnp.where(sblk < n_sel_blocks, score, -jnp.inf)
    sb = _topk_bias(score, N_SEL, sblk)
    sb = jnp.broadcast_to(sb.reshape(N_KV, 1, n_new, n_sel_lanes), (N_KV, N_REP, n_new, n_sel_lanes))
    sbias_ref[0] = sb.reshape(rows, n_sel_lanes)

    mblk = lax.broadcasted_iota(jnp.int32, (rows, n_moba_lanes), 1)
    own = _div_pow2(qpos, MOBA_BLOCK)
    gate = _dot_t(qm_ref[0], kmean_ref[0].astype(BF16))
    mscore = jnp.where(mblk == own, jnp.inf, jnp.where(mblk < own, gate, -jnp.inf))
    mbias_ref[0] = _topk_bias(mscore, MOBA_TOPK + 1, mblk)


def sample_select(qn, qm, kc, vc, kmean, slope, tpos, past, n_new, n_sel_lanes):
    b, rows, _ = qn.shape

    def spec(a):
        return pl.BlockSpec((1,) + a.shape[1:], lambda bi: (bi,) + (0,) * (a.ndim - 1))

    def full(a):
        return pl.BlockSpec(a.shape, lambda bi: (0,) * a.ndim)

    outs = [jax.ShapeDtypeStruct((b, rows, HEAD_DIM), F32),
            jax.ShapeDtypeStruct((b, rows, n_sel_lanes), F32),
            jax.ShapeDtypeStruct((b, rows, kmean.shape[1]), F32)]
    return pl.pallas_call(
        functools.partial(_sample_select_kernel, past=past, n_new=n_new),
        grid=(b,),
        in_specs=[spec(qn), spec(qm), spec(kc), spec(vc), spec(kmean), full(slope), full(tpos)],
        out_specs=[spec(o) for o in outs],
        out_shape=outs,
        compiler_params=_cparams("parallel"),
        name="sample_select",
    )(qn, qm, kc, vc, kmean, slope, tpos)


def _attn_sample_kernel(pt_ref, *refs, pages_per_step, past, n_new):
    del pt_ref
    pps = pages_per_step
    (qn_ref, qm_ref, sbias_ref, mbias_ref, slope_ref, tpos_ref) = refs[:6]
    page_refs = refs[6:6 + 4 * pps]
    (ksn_ref, vsn_ref, kmn_ref, vmn_ref, kw_ref, vw_ref, ocmp_ref, gt_ref) = refs[6 + 4 * pps:14 + 4 * pps]
    on_ref, om_ref = refs[14 + 4 * pps:16 + 4 * pps]
    ms_ref, ls_ref, as_ref, mm_ref, lm_ref, am_ref = refs[16 + 4 * pps:]
    step = pl.program_id(1)
    rows = N_HEADS * n_new
    rpg = N_REP * n_new
    qn = qn_ref[0]
    qm = qm_ref[0]
    slope = slope_ref[...]
    tpos = tpos_ref[...]

    @pl.when(step == 0)
    def _():
        ms_ref[...] = jnp.full(ms_ref.shape, NEG_BIG, F32)
        mm_ref[...] = jnp.full(mm_ref.shape, NEG_BIG, F32)
        ls_ref[...] = jnp.zeros(ls_ref.shape, F32)
        lm_ref[...] = jnp.zeros(lm_ref.shape, F32)
        as_ref[...] = jnp.zeros(as_ref.shape, F32)
        am_ref[...] = jnp.zeros(am_ref.shape, F32)

    sbias = sbias_ref[0].astype(BF16)
    mbias = mbias_ref[0].astype(BF16)
    key = lax.broadcasted_iota(jnp.int32, (1, PAGE_SIZE), 1)
    s_n = lax.broadcasted_iota(jnp.int32, (sbias.shape[1], PAGE_SIZE), 0)
    s_j = _div_pow2(lax.broadcasted_iota(jnp.int32, (sbias.shape[1], PAGE_SIZE), 1), SEL_BLOCK)
    m_n = lax.broadcasted_iota(jnp.int32, (mbias.shape[1], PAGE_SIZE), 0)
    slc = (ms_ref[...], ls_ref[...], as_ref[...])
    moba = (mm_ref[...], lm_ref[...], am_ref[...])
    for i in range(pps):
        page = step * pps + i
        alibi = slope * (page * PAGE_SIZE - past + key).astype(F32)
        s_expand = (s_n == page * (PAGE_SIZE // SEL_BLOCK) + s_j).astype(BF16)
        m_expand = (m_n == _div_pow2(page, MOBA_BLOCK // PAGE_SIZE)).astype(BF16)
        slc = _flash_update(slc, qn, page_refs[i][0].astype(BF16), page_refs[pps + i][0].astype(BF16),
                            alibi + _dot(sbias, s_expand))
        moba = _flash_update(moba, qm, page_refs[2 * pps + i][0].astype(BF16), page_refs[3 * pps + i][0].astype(BF16),
                             alibi + _dot(mbias, m_expand))
    ms_ref[...], ls_ref[...], as_ref[...] = slc
    mm_ref[...], lm_ref[...], am_ref[...] = moba

    @pl.when(step == pl.num_programs(1) - 1)
    def _():
        new = lax.broadcasted_iota(jnp.int32, (1, n_new), 1)
        bias = jnp.where(new <= tpos, slope * new.astype(F32), NEG_BIG)
        _, l, acc = _flash_update(slc, qn, ksn_ref[0].astype(BF16), vsn_ref[0].astype(BF16), bias)
        o_slc = _diag_blocks(acc / l, rpg)
        _, l, acc = _flash_update(moba, qm, kmn_ref[0].astype(BF16), vmn_ref[0].astype(BF16), bias)
        om_ref[0] = _diag_blocks(acc / l, rpg)

        n_ctx = kw_ref.shape[1]
        wrel = lax.broadcasted_iota(jnp.int32, (1, n_ctx), 1) - (n_ctx - n_new)
        dist = tpos - wrel
        valid = (dist >= 0) & (dist < WINDOW) & (wrel + past >= 0)
        s = jnp.where(valid, _dot_t(qn, kw_ref[0].astype(BF16)) + slope * wrel.astype(F32), NEG_BIG)
        pw = jnp.exp(s - jnp.max(s, axis=-1, keepdims=True))
        o_win = _diag_blocks(_dot(pw.astype(BF16), vw_ref[0].astype(BF16)) / jnp.sum(pw, axis=-1, keepdims=True), rpg)

        gates = jax.nn.sigmoid(gt_ref[0])
        on_ref[0] = gates[:, 0:1] * ocmp_ref[0] + gates[:, 1:2] * o_slc + gates[:, 2:3] * o_win


def attn_sample(page_table, layer, qn, qm, sbias, mbias, slope, tpos, pools, new_kv, kwin, vwin, ocmp, gt,
                past, n_new, pages_per_step):
    b, rows, _ = qn.shape
    n_pages = page_table.shape[1]
    n_pool = pools[0].shape[0] // DEPTH
    pps = pages_per_step
    assert n_pages % pps == 0 and n_pages * PAGE_SIZE == past

    def spec(a):
        return pl.BlockSpec((1,) + a.shape[1:], lambda bi, s, pt: (bi,) + (0,) * (a.ndim - 1))

    def full(a):
        return pl.BlockSpec(a.shape, lambda bi, s, pt: (0,) * a.ndim)

    def page_spec(i):
        return pl.BlockSpec((1, PAGE_SIZE, KV_W),
                            lambda bi, s, pt: (layer * n_pool + pt[bi * n_pages + s * pps + i], 0, 0))

    page_specs = [page_spec(i) for _ in range(4) for i in range(pps)]
    page_args = [pool for pool in pools for _ in range(pps)]
    o_shape = jax.ShapeDtypeStruct((b, rows, HEAD_DIM), F32)
    return pl.pallas_call(
        functools.partial(_attn_sample_kernel, pages_per_step=pps, past=past, n_new=n_new),
        grid_spec=pltpu.PrefetchScalarGridSpec(
            num_scalar_prefetch=1,
            grid=(b, n_pages // pps),
            in_specs=[spec(qn), spec(qm), spec(sbias), spec(mbias), full(slope), full(tpos)] + page_specs
                     + [spec(a) for a in new_kv] + [spec(kwin), spec(vwin), spec(ocmp), spec(gt)],
            out_specs=[spec(o_shape), spec(o_shape)],
            scratch_shapes=[pltpu.VMEM((rows, 1), F32), pltpu.VMEM((rows, 1), F32), pltpu.VMEM((rows, KV_W), F32),
                            pltpu.VMEM((rows, 1), F32), pltpu.VMEM((rows, 1), F32), pltpu.VMEM((rows, KV_W), F32)],
        ),
        out_shape=[o_shape, o_shape],
        compiler_params=_cparams("parallel", "arbitrary"),
        name="attn_sample",
    )(page_table.reshape(-1), qn, qm, sbias, mbias, slope, tpos, *page_args, *new_kv, kwin, vwin, ocmp, gt)


def _out_proj_kernel(on_ref, om_ref, gn_ref, gm_ref, wn_ref, wm_ref, x_ref, g_ref, b_ref, o_ref):
    hn = _rms_norm(on_ref[...], gn_ref[...]).astype(BF16)
    hm = _rms_norm(om_ref[...], gm_ref[...]).astype(BF16)
    y = _dot(hn, wn_ref[...]) + _dot(hm, wm_ref[...])
    o_ref[...] = _layer_norm(DEEPNORM_ALPHA * x_ref[...] + y, g_ref[...], b_ref[...])


def out_proj_ln(o_nsa, o_moba, gn_nsa, gn_moba, w_nsa, w_moba, x, g, b):
    m = x.shape[0]
    assert m % ROW_TILE == 0
    row = lambda w: pl.BlockSpec((ROW_TILE, w), lambda i: (i, 0))
    full = lambda a: pl.BlockSpec(a.shape, lambda i: (0,) * a.ndim)
    return pl.pallas_call(
        _out_proj_kernel,
        grid=(m // ROW_TILE,),
        in_specs=[row(Q_W), row(Q_W), full(gn_nsa), full(gn_moba), full(w_nsa), full(w_moba),
                  row(D_MODEL), full(g), full(b)],
        out_specs=row(D_MODEL),
        out_shape=jax.ShapeDtypeStruct((m, D_MODEL), F32),
        compiler_params=_cparams("parallel"),
        name="out_proj_ln",
    )(o_nsa, o_moba, gn_nsa, gn_moba, w_nsa, w_moba, x, g, b)


def _mem_attn_kernel(x_ref, wq_ref, k_ref, v_ref, wo_ref, g_ref, b_ref, o_ref):
    x = x_ref[0]
    q = _dot(x.astype(BF16), wq_ref[...]).astype(BF16)
    k = k_ref[0].astype(BF16)
    v = v_ref[0].astype(BF16)
    heads = []
    for h in range(MEM_HEADS):
        sl = slice(h * MEM_HEAD_DIM, (h + 1) * MEM_HEAD_DIM)
        s = _dot_t(q[:, sl], k[:, sl]) * (MEM_HEAD_DIM ** -0.5)
        e = jnp.exp(s - jnp.max(s, axis=-1, keepdims=True))
        p = e / jnp.sum(e, axis=-1, keepdims=True)
        heads.append(_dot(p.astype(BF16), v[:, sl]))
    o = jnp.concatenate(heads, axis=-1).astype(BF16)
    o_ref[0] = _layer_norm(DEEPNORM_ALPHA * x + _dot(o, wo_ref[...]), g_ref[...], b_ref[...])


def mem_attn_ln(x, wq, mem_k, mem_v, wo, g, b, tq):
    bsz, t, d = x.shape
    n_mem, d_mem = mem_k.shape[1:]
    assert t % tq == 0
    full = lambda a: pl.BlockSpec(a.shape, lambda bi, i: (0,) * a.ndim)
    return pl.pallas_call(
        _mem_attn_kernel,
        grid=(bsz, t // tq),
        in_specs=[pl.BlockSpec((1, tq, d), lambda bi, i: (bi, i, 0)), full(wq),
                  pl.BlockSpec((1, n_mem, d_mem), lambda bi, i: (bi, 0, 0)),
                  pl.BlockSpec((1, n_mem, d_mem), lambda bi, i: (bi, 0, 0)), full(wo), full(g), full(b)],
        out_specs=pl.BlockSpec((1, tq, d), lambda bi, i: (bi, i, 0)),
        out_shape=jax.ShapeDtypeStruct(x.shape, F32),
        compiler_params=_cparams("parallel", "arbitrary"),
        name="mem_attn_ln",
    )(x, wq, mem_k, mem_v, wo, g, b)


def _router_kernel(x_ref, w_ref, b_ref, e_ref, g_ref):
    logits = jnp.dot(x_ref[...], w_ref[...], preferred_element_type=F32, precision=lax.Precision.HIGHEST) + b_ref[...]
    lane = lax.broadcasted_iota(jnp.int32, logits.shape, 1).astype(F32)
    cur = logits
    vals, idxs = [], []
    for _ in range(TOP_K):
        m = jnp.max(cur, axis=-1, keepdims=True)
        idx = jnp.min(jnp.where(cur == m, lane, float(N_EXPERTS)), axis=-1, keepdims=True)
        vals.append(m)
        idxs.append(idx)
        cur = jnp.where(lane == idx, -jnp.inf, cur)
    top_v = jnp.concatenate(vals, axis=-1)
    e = jnp.exp(top_v - vals[0])
    g_ref[...] = e / jnp.sum(e, axis=-1, keepdims=True)
    e_ref[...] = jnp.concatenate(idxs, axis=-1).astype(jnp.int32)


def router(x, w, b):
    m, d = x.shape
    assert m % ROW_TILE == 0
    return pl.pallas_call(
        _router_kernel,
        grid=(m // ROW_TILE,),
        in_specs=[pl.BlockSpec((ROW_TILE, d), lambda i: (i, 0)),
                  pl.BlockSpec(w.shape, lambda i: (0, 0)), pl.BlockSpec(b.shape, lambda i: (0, 0))],
        out_specs=[pl.BlockSpec((ROW_TILE, TOP_K), lambda i: (i, 0)), pl.BlockSpec((ROW_TILE, TOP_K), lambda i: (i, 0))],
        out_shape=[jax.ShapeDtypeStruct((m, TOP_K), jnp.int32), jax.ShapeDtypeStruct((m, TOP_K), F32)],
        compiler_params=_cparams("parallel"),
        name="router",
    )(x, w, b)


def _expert_up_kernel(be_ref, nu_ref, x_ref, wg_ref, wu_ref, bg_ref, bu_ref, h_ref):
    del be_ref

    @pl.when(pl.program_id(1) < nu_ref[0])
    def _():
        x = x_ref[...]
        g = _dot(x, wg_ref[0].astype(BF16)) + bg_ref[0]
        u = _dot(x, wu_ref[0].astype(BF16)) + bu_ref[0]
        g = jnp.minimum(g, SWIGLU_LIMIT)
        u = jnp.clip(u, -SWIGLU_LIMIT, SWIGLU_LIMIT)
        h_ref[...] = (g * jax.nn.sigmoid(SWIGLU_ALPHA * g) * (u + 1.0)).astype(BF16)

    @pl.when(pl.program_id(1) >= nu_ref[0])
    def _():
        h_ref[...] = jnp.zeros(h_ref.shape, BF16)


def _expert_down_kernel(be_ref, nu_ref, h_ref, wd_ref, bd_ref, y_ref):
    del be_ref

    @pl.when(pl.program_id(1) < nu_ref[0])
    def _():
        y_ref[...] = _dot(h_ref[...], wd_ref[0].astype(BF16)) + bd_ref[0]

    @pl.when(pl.program_id(1) >= nu_ref[0])
    def _():
        y_ref[...] = jnp.zeros(y_ref.shape, F32)


def expert_ffn(xs, blk_e, n_used, layer, wg, bg, wu, bu, wd, bd):
    cap, d = xs.shape
    n_blocks = cap // MOE_TILE
    d_ff = wg.shape[-1]
    we = lambda j, i, be, nu: layer * N_EXPERTS + be[i]
    h = pl.pallas_call(
        _expert_up_kernel,
        grid_spec=pltpu.PrefetchScalarGridSpec(
            num_scalar_prefetch=2,
            grid=(d_ff // FF_CHUNK, n_blocks),
            in_specs=[pl.BlockSpec((MOE_TILE, d), lambda j, i, be, nu: (i, 0)),
                      pl.BlockSpec((1, d, FF_CHUNK), lambda j, i, be, nu: (we(j, i, be, nu), 0, j)),
                      pl.BlockSpec((1, d, FF_CHUNK), lambda j, i, be, nu: (we(j, i, be, nu), 0, j)),
                      pl.BlockSpec((1, 1, FF_CHUNK), lambda j, i, be, nu: (we(j, i, be, nu), 0, j)),
                      pl.BlockSpec((1, 1, FF_CHUNK), lambda j, i, be, nu: (we(j, i, be, nu), 0, j))],
            out_specs=pl.BlockSpec((MOE_TILE, FF_CHUNK), lambda j, i, be, nu: (i, j)),
        ),
        out_shape=jax.ShapeDtypeStruct((cap, d_ff), BF16),
        compiler_params=_cparams("parallel", "arbitrary"),
        name="expert_up",
    )(blk_e, n_used, xs, wg, wu, bg, bu)
    return pl.pallas_call(
        _expert_down_kernel,
        grid_spec=pltpu.PrefetchScalarGridSpec(
            num_scalar_prefetch=2,
            grid=(d // FF_CHUNK, n_blocks),
            in_specs=[pl.BlockSpec((MOE_TILE, d_ff), lambda j, i, be, nu: (i, 0)),
                      pl.BlockSpec((1, d_ff, FF_CHUNK), lambda j, i, be, nu: (we(j, i, be, nu), 0, j)),
                      pl.BlockSpec((1, 1, FF_CHUNK), lambda j, i, be, nu: (we(j, i, be, nu), 0, j))],
            out_specs=pl.BlockSpec((MOE_TILE, FF_CHUNK), lambda j, i, be, nu: (i, j)),
        ),
        out_shape=jax.ShapeDtypeStruct((cap, d), F32),
        compiler_params=_cparams("parallel", "arbitrary"),
        name="expert_down",
    )(blk_e, n_used, h, wd, bd)


def _residual_ln_kernel(x_ref, m_ref, g_ref, b_ref, o_ref):
    o_ref[...] = _layer_norm(DEEPNORM_ALPHA * x_ref[...] + m_ref[...], g_ref[...], b_ref[...])


def residual_ln(x, m, g, b):
    n, d = x.shape
    row = pl.BlockSpec((ROW_TILE, d), lambda i: (i, 0))
    full = lambda a: pl.BlockSpec(a.shape, lambda i: (0,) * a.ndim)
    return pl.pallas_call(
        _residual_ln_kernel,
        grid=(n // ROW_TILE,),
        in_specs=[row, row, full(g), full(b)],
        out_specs=row,
        out_shape=jax.ShapeDtypeStruct((n, d), F32),
        compiler_params=_cparams("parallel"),
        name="residual_ln",
    )(x, m, g, b)


def moe_ffn(x, layer, router_w, router_b, wg, bg, wu, bu, wd, bd):
    n, d = x.shape
    top_e, gate = router(x, router_w, router_b)
    a = n * TOP_K
    e_flat = top_e.reshape(a)
    onehot = (e_flat[:, None] == jnp.arange(N_EXPERTS, dtype=jnp.int32)[None, :]).astype(jnp.int32)
    csum = jnp.cumsum(onehot, axis=0)
    rank = jnp.sum((csum - onehot) * onehot, axis=1)
    counts = csum[-1]
    padded = (counts + MOE_TILE - 1) // MOE_TILE * MOE_TILE
    pad_end = jnp.cumsum(padded)
    pad_start = pad_end - padded
    dest = pad_start[e_flat] + rank
    n_blocks = -(-a // MOE_TILE) + N_EXPERTS
    cap = n_blocks * MOE_TILE
    tok = jnp.arange(a, dtype=jnp.int32) // TOP_K
    src = jnp.full((cap,), n, jnp.int32).at[dest].set(tok)
    blk_start = jnp.arange(n_blocks, dtype=jnp.int32) * MOE_TILE
    blk_e = jnp.minimum(jnp.sum(blk_start[:, None] >= pad_end[None, :], axis=1), N_EXPERTS - 1).astype(jnp.int32)
    n_used = (pad_end[-1] // MOE_TILE).astype(jnp.int32).reshape(1)
    x_pad = jnp.concatenate([x.astype(BF16), jnp.zeros((1, d), BF16)], axis=0)
    xs = x_pad[src]
    y = expert_ffn(xs, blk_e, n_used, layer, wg, bg, wu, bu, wd, bd)
    return jnp.sum(y[dest].reshape(n, TOP_K, d) * gate[:, :, None], axis=1)


def _alibi_slopes():
    return jnp.asarray(2.0 ** (-8.0 * np.arange(1, N_HEADS + 1) / N_HEADS), F32)


def _prep_w_in(w):
    q_n, kv_n, gt, q_m, kv_m = jnp.split(w, [Q_W, Q_W + 6 * KV_W, Q_W + 6 * KV_W + 3 * N_HEADS,
                                            2 * Q_W + 6 * KV_W + 3 * N_HEADS], axis=1)
    main = jnp.concatenate([q_n, kv_n, q_m, kv_m], axis=1).astype(BF16)
    gates = jnp.pad(gt, ((0, 0), (0, 128 - 3 * N_HEADS))).astype(BF16)
    return main, gates


def _heads_major(q, b, t):
    q = (q * (HEAD_DIM ** -0.5)).astype(BF16)
    return q.reshape(b, t, N_KV, N_REP, HEAD_DIM).transpose(0, 2, 3, 1, 4)


def _group_major(k, b, t):
    return k.reshape(b, t, N_KV, HEAD_DIM).transpose(0, 2, 1, 3)


def _block_diag_rows(q, b, t):
    q = (q * (HEAD_DIM ** -0.5)).astype(BF16).reshape(b, t, N_KV, N_REP, HEAD_DIM).transpose(0, 2, 3, 1, 4)
    eye = jnp.eye(N_KV, dtype=BF16)
    bd = q[:, :, :, :, None, :] * eye[None, :, None, None, :, None]
    return bd.reshape(b, N_HEADS * t, KV_W)


def _rows_to_tokens(o, b, t):
    return o.reshape(b, N_KV, N_REP, t, HEAD_DIM).transpose(0, 3, 1, 2, 4).reshape(b * t, Q_W)


def kernel(x_prompt, x_sample, cache_nsa_cmp_k, cache_nsa_cmp_v, cache_nsa_slc_k, cache_nsa_slc_v, cache_moba_k, cache_moba_v, cache_nsa_win_k, cache_nsa_win_v, cache_mem_k, cache_mem_v, page_table, mem_prompt, w_in, nsa_cmp_pos, nsa_cmp_k_w1, nsa_cmp_k_w2, nsa_cmp_v_w1, nsa_cmp_v_w2, gn_nsa, gn_moba, w_out, ln1_g, ln1_b, ca_wq, ca_wk, ca_wv, ca_wo, ln2_g, ln2_b, router_w, router_b, exp_wg, exp_bg, exp_wu, exp_bu, exp_wd, exp_bd, ln3_g, ln3_b):
    bp, tp, d = x_prompt.shape
    bs, ts, _ = x_sample.shape
    n_p, n_s = bp * tp, bs * ts
    n_mem = mem_prompt.shape[1]
    depth, n_pool = cache_nsa_cmp_k.shape[:2]
    n_pages = page_table.shape[1]
    past = n_pages * PAGE_SIZE
    wbuf = cache_nsa_win_k.shape[2]
    d_mem = MEM_HEADS * MEM_HEAD_DIM
    slopes = _alibi_slopes()
    pool = lambda c: c.reshape(depth * n_pool, PAGE_SIZE, KV_W)
    slc_k_pool, slc_v_pool, moba_k_pool, moba_v_pool = map(pool, (cache_nsa_slc_k, cache_nsa_slc_v, cache_moba_k, cache_moba_v))
    wg = exp_wg.reshape((depth * N_EXPERTS,) + exp_wg.shape[2:])
    wu = exp_wu.reshape((depth * N_EXPERTS,) + exp_wu.shape[2:])
    wd = exp_wd.reshape((depth * N_EXPERTS,) + exp_wd.shape[2:])
    bg = exp_bg.reshape(depth * N_EXPERTS, 1, -1)
    bu = exp_bu.reshape(depth * N_EXPERTS, 1, -1)
    bd = exp_bd.reshape(depth * N_EXPERTS, 1, -1)

    row = np.arange(N_HEADS * ts)
    s_slope = slopes[row // ts].reshape(-1, 1)
    s_tpos = jnp.asarray((row % ts).reshape(-1, 1), jnp.int32)
    n_cmp_s = (past + ts) // CMP_BLOCK
    n_sel_lanes = -(-(-(-(past + ts) // SEL_BLOCK)) // 128) * 128
    n_moba_s = -(-(past + ts) // MOBA_BLOCK)

    x = jnp.concatenate([x_prompt.reshape(n_p, d), x_sample.reshape(n_s, d)], axis=0)
    prompt_new = [[] for _ in range(10)]
    sample_new = [[] for _ in range(8)]
    for l in range(depth):
        w_main, w_gate = _prep_w_in(w_in[l])
        proj = matmul(x, w_main, 1280, 512)
        gt = matmul(x, w_gate, 1280, 128)[:, :3 * N_HEADS]
        q_n, kc, vc, ks, vs, kw, vw, q_m, km, vm = jnp.split(
            proj, [Q_W + KV_W * i for i in range(7)] + [2 * Q_W + 6 * KV_W, 2 * Q_W + 7 * KV_W], axis=1)
        pos = nsa_cmp_pos[l].reshape(1, CMP_BLOCK * HEAD_DIM)
        ck_w1, ck_w2 = nsa_cmp_k_w1[l].astype(BF16), nsa_cmp_k_w2[l].astype(BF16)
        cv_w1, cv_w2 = nsa_cmp_v_w1[l].astype(BF16), nsa_cmp_v_w2[l].astype(BF16)

        gm = lambda a: _group_major(a[:n_p], bp, tp)
        kc_rows = gm(kc).reshape(bp * N_KV * (tp // CMP_BLOCK), CMP_BLOCK * HEAD_DIM)
        vc_rows = gm(vc).reshape(bp * N_KV * (tp // CMP_BLOCK), CMP_BLOCK * HEAD_DIM)
        k_cmp = cmp_mlp(kc_rows, pos, ck_w1, ck_w2, 256).reshape(bp, N_KV, tp // CMP_BLOCK, HEAD_DIM)
        v_cmp = cmp_mlp(vc_rows, pos, cv_w1, cv_w2, 256).reshape(bp, N_KV, tp // CMP_BLOCK, HEAD_DIM)
        gt_p = gt[:n_p].reshape(bp, tp, N_KV, 3 * N_REP).transpose(0, 2, 1, 3)
        o_nsa_p, o_moba_p = attn_prompt(
            slopes, _heads_major(q_n[:n_p], bp, tp), _heads_major(q_m[:n_p], bp, tp), k_cmp, v_cmp,
            *[gm(a).astype(BF16) for a in (ks, vs, kw, vw, km, vm)], gt_p)

        sn = lambda a: a[n_p:].reshape(bs, ts, KV_W)
        def cmp_rows(cache):
            g = cache[l][page_table].reshape(bs, n_pages * PAGE_SIZE // CMP_BLOCK, CMP_BLOCK, N_KV, HEAD_DIM)
            return g.transpose(0, 1, 3, 2, 4).reshape(bs * n_cmp_s * N_KV, CMP_BLOCK * HEAD_DIM)
        k_cmp_s = cmp_mlp(cmp_rows(cache_nsa_cmp_k), pos, ck_w1, ck_w2, 256).reshape(bs, n_cmp_s, KV_W)
        v_cmp_s = cmp_mlp(cmp_rows(cache_nsa_cmp_v), pos, cv_w1, cv_w2, 256).reshape(bs, n_cmp_s, KV_W)
        psum = page_sums(moba_k_pool, l, page_table, 8)
        per_blk = MOBA_BLOCK // PAGE_SIZE
        k_mean = psum.reshape(bs, n_pages // per_blk, per_blk, KV_W).sum(axis=2)
        k_mean = jnp.concatenate([k_mean, sn(km).sum(axis=1, keepdims=True)], axis=1) * (1.0 / MOBA_BLOCK)
        k_mean = jnp.pad(k_mean, ((0, 0), (0, 128 - n_moba_s), (0, 0)))
        qn_s = _block_diag_rows(q_n[n_p:], bs, ts)
        qm_s = _block_diag_rows(q_m[n_p:], bs, ts)
        o_cmp_s, sbias, mbias = sample_select(qn_s, qm_s, k_cmp_s, v_cmp_s, k_mean, s_slope, s_tpos, past, ts, n_sel_lanes)
        kwin = jnp.concatenate([cache_nsa_win_k[l].reshape(bs, wbuf, KV_W), sn(kw)], axis=1)
        vwin = jnp.concatenate([cache_nsa_win_v[l].reshape(bs, wbuf, KV_W), sn(vw)], axis=1)
        gt_s = gt[n_p:].reshape(bs, ts, N_KV, N_REP, 3).transpose(0, 2, 3, 1, 4).reshape(bs, N_HEADS * ts, 3)
        o_nsa_s, o_moba_s = attn_sample(
            page_table, l, qn_s, qm_s, sbias, mbias, s_slope, s_tpos,
            (slc_k_pool, slc_v_pool, moba_k_pool, moba_v_pool), (sn(ks), sn(vs), sn(km), sn(vm)),
            kwin, vwin, o_cmp_s, gt_s, past, ts, 8)

        o_nsa = jnp.concatenate([o_nsa_p.reshape(n_p, Q_W), _rows_to_tokens(o_nsa_s, bs, ts)], axis=0)
        o_moba = jnp.concatenate([o_moba_p.reshape(n_p, Q_W), _rows_to_tokens(o_moba_s, bs, ts)], axis=0)
        w_o = w_out[l].astype(BF16)
        x = out_proj_ln(o_nsa, o_moba, gn_nsa[l].reshape(1, -1), gn_moba[l].reshape(1, -1), w_o[:Q_W], w_o[Q_W:],
                        x, ln1_g[l].reshape(1, -1), ln1_b[l].reshape(1, -1))

        mem_kv = matmul(mem_prompt.reshape(bp * n_mem, d), jnp.concatenate([ca_wk[l], ca_wv[l]], axis=1).astype(BF16), 512, 512)
        mem_k, mem_v = mem_kv[:, :d_mem].reshape(bp, n_mem, d_mem), mem_kv[:, d_mem:].reshape(bp, n_mem, d_mem)
        wq, wo = ca_wq[l].astype(BF16), ca_wo[l].astype(BF16)
        g2, b2 = ln2_g[l].reshape(1, -1), ln2_b[l].reshape(1, -1)
        x_p = mem_attn_ln(x[:n_p].reshape(bp, tp, d), wq, mem_k, mem_v, wo, g2, b2, ROW_TILE)
        x_s = mem_attn_ln(x[n_p:].reshape(bs, ts, d), wq, cache_mem_k[l].reshape(bs, -1, d_mem),
                          cache_mem_v[l].reshape(bs, -1, d_mem), wo, g2, b2, ts)
        x = jnp.concatenate([x_p.reshape(n_p, d), x_s.reshape(n_s, d)], axis=0)

        m = moe_ffn(x, l, router_w[l], router_b[l].reshape(1, -1), wg, bg, wu, bu, wd, bd)
        x = residual_ln(x, m, ln3_g[l].reshape(1, -1), ln3_b[l].reshape(1, -1))

        st4 = lambda a, b, t: a.reshape(b, t, N_KV, HEAD_DIM)
        keep = min(WINDOW, tp)
        p_state = [st4(a[:n_p], bp, tp) for a in (kc, vc, ks, vs, km, vm)]
        p_state += [st4(kw[:n_p], bp, tp)[:, -keep:], st4(vw[:n_p], bp, tp)[:, -keep:],
                    mem_k.reshape(bp, n_mem, MEM_HEADS, MEM_HEAD_DIM), mem_v.reshape(bp, n_mem, MEM_HEADS, MEM_HEAD_DIM)]
        keep_s = min(WINDOW, wbuf + ts)
        s_state = [st4(a[n_p:], bs, ts) for a in (kc, vc, ks, vs, km, vm)]
        s_state += [kwin[:, -keep_s:].reshape(bs, keep_s, N_KV, HEAD_DIM), vwin[:, -keep_s:].reshape(bs, keep_s, N_KV, HEAD_DIM)]
        for acc, a in zip(prompt_new, p_state):
            acc.append(a)
        for acc, a in zip(sample_new, s_state):
            acc.append(a)

    y_prompt = x[:n_p].reshape(bp, tp, d)
    y_sample = x[n_p:].reshape(bs, ts, d)
    return (y_prompt, y_sample) + tuple(jnp.stack(a) for a in prompt_new) + tuple(jnp.stack(a) for a in sample_new)
```

```python
import functools

import numpy as np
import jax
import jax.numpy as jnp
from jax import lax
from jax.experimental import pallas as pl
from jax.experimental.pallas import tpu as pltpu

F32 = jnp.float32
BF16 = jnp.bfloat16

D_MODEL = 2048
HEAD_DIM = 64
N_KV = 4
N_REP = 4
N_HEADS = N_KV * N_REP
CMP_BLOCK = 64
SEL_BLOCK = 64
N_SEL = 8
WINDOW = 512
MOBA_BLOCK = 256
MOBA_TOPK = 3
MEM_HEADS = 4
MEM_HEAD_DIM = 128
N_EXPERTS = 32
TOP_K = 4
SWIGLU_LIMIT = 7.0
SWIGLU_ALPHA = 1.702
LN_EPS = 1e-5
NEG_BIG = -1e30
DEPTH = 2
DEEPNORM_ALPHA = (2 * DEPTH) ** 0.25
PAGE_SIZE = 128

KV_W = N_KV * HEAD_DIM
Q_W = N_HEADS * HEAD_DIM
N_STATE = 8
GATE_PAD = 16
ALIBI_PARTS = 3
TQ = 256
VMEM_LIMIT = 56 * 1024 * 1024
ROW_TILE = 256
PROJ_TILE = 512
MOE_TILE = 512
FF_CHUNK = 512


def _cparams(*sem):
    return pltpu.CompilerParams(dimension_semantics=sem, vmem_limit_bytes=VMEM_LIMIT)


def _dot(a, b):
    return jnp.dot(a, b, preferred_element_type=F32)


def _dot_t(a, b):
    return lax.dot_general(a, b, (((1,), (1,)), ((), ())), preferred_element_type=F32)


def _div_pow2(x, n):
    assert n & (n - 1) == 0
    return lax.shift_right_logical(x, n.bit_length() - 1)


def _layer_norm(z, g, b):
    mu = jnp.mean(z, axis=-1, keepdims=True)
    zc = z - mu
    var = jnp.mean(zc * zc, axis=-1, keepdims=True)
    return zc * lax.rsqrt(var + LN_EPS) * g + b


def _rms_norm(x, g):
    return x * lax.rsqrt(jnp.mean(x * x, axis=-1, keepdims=True) + LN_EPS) * g


def _masked_softmax(s, mask, axis):
    s = jnp.where(mask, s, NEG_BIG)
    m = jnp.max(s, axis=axis, keepdims=True)
    e = jnp.where(mask, jnp.exp(s - m), 0.0)
    den = jnp.sum(e, axis=axis, keepdims=True)
    return e / jnp.where(den > 0.0, den, 1.0)


def _topk_bias(score, k, index, axis):
    index = index.astype(F32)
    cur = score
    bias = jnp.full(score.shape, NEG_BIG, F32)
    for _ in range(k):
        m = jnp.max(cur, axis=axis, keepdims=True)
        first = jnp.min(jnp.where(cur == m, index, float(score.shape[axis])), axis=axis, keepdims=True)
        pick = index == first
        bias = jnp.where(pick & (m > -jnp.inf), 0.0, bias)
        cur = jnp.where(pick, -jnp.inf, cur)
    return bias


def _mm_kernel(x_ref, w_ref, o_ref):
    o_ref[...] = _dot(x_ref[...].astype(BF16), w_ref[...])


def matmul(x, w, tm, tn):
    m, k = x.shape
    n = w.shape[1]
    tm = max(t for t in range(8, min(tm, m) + 1, 8) if m % t == 0)
    assert n % tn == 0
    return pl.pallas_call(
        _mm_kernel,
        grid=(m // tm, n // tn),
        in_specs=[pl.BlockSpec((tm, k), lambda i, j: (i, 0)),
                  pl.BlockSpec((k, tn), lambda i, j: (0, j))],
        out_specs=pl.BlockSpec((tm, tn), lambda i, j: (i, j)),
        out_shape=jax.ShapeDtypeStruct((m, n), F32),
        compiler_params=_cparams("parallel", "arbitrary"),
        name="matmul",
    )(x, w)


def _proj_fm_kernel(x_ref, w_ref, *o_refs, scale):
    r = _dot_t(w_ref[...], x_ref[0].astype(BF16))
    if scale != 1.0:
        r = r * scale
    for o_ref in o_refs:
        o_ref[0] = r.astype(o_ref.dtype)


def proj_feature_major(x, w_t, dtypes, scale=1.0):
    b, t, k = x.shape
    f = w_t.shape[0]
    tf = min(PROJ_TILE, f)
    assert t % PROJ_TILE == 0 and f % tf == 0
    o_spec = pl.BlockSpec((1, tf, PROJ_TILE), lambda bi, j, fi: (bi, fi, j))
    return pl.pallas_call(
        functools.partial(_proj_fm_kernel, scale=scale),
        grid=(b, t // PROJ_TILE, f // tf),
        in_specs=[pl.BlockSpec((1, PROJ_TILE, k), lambda bi, j, fi: (bi, j, 0)),
                  pl.BlockSpec((tf, k), lambda bi, j, fi: (fi, 0))],
        out_specs=[o_spec for _ in dtypes],
        out_shape=[jax.ShapeDtypeStruct((b, f, t), dt) for dt in dtypes],
        compiler_params=_cparams("parallel", "parallel", "arbitrary"),
        name="proj_feature_major",
    )(x, w_t)


def _proj_keys_kernel(x_ref, w_ref, kc_ref, vc_ref, k_ref):
    r = _dot(x_ref[0].astype(BF16), w_ref[...])
    for g in range(N_KV):
        lo = g * HEAD_DIM
        kc_ref[0, g] = r[:, lo:lo + HEAD_DIM]
        vc_ref[0, g] = r[:, KV_W + lo:KV_W + lo + HEAD_DIM]
        for w in range(3):
            k_ref[0, w, g] = r[:, (2 + w) * KV_W + lo:(2 + w) * KV_W + lo + HEAD_DIM].astype(BF16)


def proj_keys_token_major(x, w):
    b, t, k = x.shape
    assert t % PROJ_TILE == 0 and w.shape[1] == 5 * KV_W
    c_spec = pl.BlockSpec((1, N_KV, PROJ_TILE, HEAD_DIM), lambda bi, j: (bi, 0, j, 0))
    c_shape = jax.ShapeDtypeStruct((b, N_KV, t, HEAD_DIM), F32)
    return pl.pallas_call(
        _proj_keys_kernel,
        grid=(b, t // PROJ_TILE),
        in_specs=[pl.BlockSpec((1, PROJ_TILE, k), lambda bi, j: (bi, j, 0)),
                  pl.BlockSpec(w.shape, lambda bi, j: (0, 0))],
        out_specs=[c_spec, c_spec, pl.BlockSpec((1, 3, N_KV, PROJ_TILE, HEAD_DIM), lambda bi, j: (bi, 0, 0, j, 0))],
        out_shape=[c_shape, c_shape, jax.ShapeDtypeStruct((b, 3, N_KV, t, HEAD_DIM), BF16)],
        compiler_params=_cparams("parallel", "arbitrary"),
        name="proj_keys_token_major",
    )(x, w)


def _cmp_mlp_kernel(x_ref, pos_ref, w1_ref, w2_ref, o_ref):
    x = (x_ref[...] + pos_ref[...]).astype(BF16)
    h = _dot(x, w1_ref[...])
    h = h * jax.nn.sigmoid(h)
    o_ref[...] = _dot(h.astype(BF16), w2_ref[...])


def cmp_mlp(x, pos, w1, w2, tr):
    r, k = x.shape
    hid = w1.shape[1]
    tr = min(tr, r)
    assert r % tr == 0
    return pl.pallas_call(
        _cmp_mlp_kernel,
        grid=(r // tr,),
        in_specs=[pl.BlockSpec((tr, k), lambda i: (i, 0)),
                  pl.BlockSpec((1, k), lambda i: (0, 0)),
                  pl.BlockSpec((k, hid), lambda i: (0, 0)),
                  pl.BlockSpec((hid, HEAD_DIM), lambda i: (0, 0))],
        out_specs=pl.BlockSpec((tr, HEAD_DIM), lambda i: (i, 0)),
        out_shape=jax.ShapeDtypeStruct((r, HEAD_DIM), F32),
        compiler_params=_cparams("parallel"),
        name="cmp_mlp",
    )(x, pos, w1, w2)


def _flash_t(carry, k_tile, q_aug, vt_tile, bias, mask):
    m, l, acc = carry
    s = _dot(k_tile, q_aug)
    if bias is not None:
        s = s + bias
    if mask is not None:
        s = jnp.where(mask, s, NEG_BIG)
    m_new = jnp.maximum(m, jnp.max(s, axis=0, keepdims=True))
    alpha = jnp.exp(m - m_new)
    p = jnp.exp(s - m_new)
    l = alpha * l + jnp.sum(p, axis=0, keepdims=True)
    acc = alpha * acc + _dot(vt_tile, p.astype(BF16))
    return m_new, l, acc


def _attn_prompt_kernel(slopes_ref, qn_ref, qm_ref, ks_ref, kw_ref, km_ref, vs_ref, vw_ref, vm_ref,
                        kc_ref, vct_ref, gt_ref, on_ref, om_ref, kaug_ref, kmean_ref, sel_ref, mb_ref, *, seq):
    g = pl.program_id(1)
    qt = pl.program_id(2)
    n_cmp = seq // CMP_BLOCK
    n_moba = seq // MOBA_BLOCK
    blocks_per_tile = TQ // SEL_BLOCK
    t0 = qt * TQ

    @pl.when(qt == 0)
    def _():
        pos = lax.broadcasted_iota(jnp.int32, (seq, HEAD_DIM), 0).astype(F32)
        lane = lax.broadcasted_iota(jnp.int32, (seq, HEAD_DIM), 1)
        cols = jnp.zeros((seq, HEAD_DIM), F32)
        for r in range(N_REP):
            v = pos * slopes_ref[g * N_REP + r]
            hi = v.astype(BF16).astype(F32)
            mid = (v - hi).astype(BF16).astype(F32)
            lo = v - hi - mid
            c = ALIBI_PARTS * r
            cols = jnp.where(lane == c, hi, jnp.where(lane == c + 1, mid, jnp.where(lane == c + 2, lo, cols)))
        cols = cols.astype(BF16)
        for w, k_ref in enumerate((ks_ref, kw_ref, km_ref)):
            kaug_ref[w] = jnp.concatenate([k_ref[0, 0, 0], cols], axis=1)
        km = km_ref[0, 0, 0].astype(F32)
        kmean_ref[...] = jnp.mean(km.reshape(n_moba, MOBA_BLOCK, HEAD_DIM), axis=1)

    wide = N_REP * TQ
    per_rep = lambda f: jnp.concatenate([f(r) for r in range(N_REP)], axis=1)
    q_in_tile = lax.broadcasted_iota(jnp.int32, (1, wide), 1) & (TQ - 1)
    tq = t0 + q_in_tile
    causal = lax.broadcasted_iota(jnp.int32, (TQ, wide), 0) <= q_in_tile
    gates = jax.nn.sigmoid(gt_ref[0])
    gate_row = lambda branch: per_rep(lambda r: gates[3 * r + branch:3 * r + branch + 1, :])
    slope_row = per_rep(lambda r: jnp.full((1, TQ), slopes_ref[g * N_REP + r], F32))
    aug_row = lax.broadcasted_iota(jnp.int32, (HEAD_DIM, wide), 0)
    aug_lo = ALIBI_PARTS * _div_pow2(lax.broadcasted_iota(jnp.int32, (HEAD_DIM, wide), 1), TQ)
    ones = jnp.where((aug_row >= aug_lo) & (aug_row < aug_lo + ALIBI_PARTS), 1.0, 0.0).astype(BF16)
    qn = per_rep(lambda r: qn_ref[0, r * HEAD_DIM:(r + 1) * HEAD_DIM, :])
    qm = per_rep(lambda r: qm_ref[0, r * HEAD_DIM:(r + 1) * HEAD_DIM, :])
    qn_aug = jnp.concatenate([qn, ones], axis=0)
    qm_aug = jnp.concatenate([qm, ones], axis=0)
    init = (jnp.full((1, wide), NEG_BIG, F32), jnp.zeros((1, wide), F32), jnp.zeros((HEAD_DIM, wide), F32))

    kc = kc_ref[0, 0].astype(BF16)
    vct = vct_ref[0, 0].astype(BF16)
    k_end = lax.broadcasted_iota(jnp.int32, (n_cmp, wide), 0) * CMP_BLOCK + (CMP_BLOCK - 1)
    p = _masked_softmax(_dot(kc, qn) + slope_row * k_end.astype(F32), k_end <= tq, 0)
    o_cmp = _dot(vct, p.astype(BF16))
    imp = p[:, 0:TQ]
    for r in range(1, N_REP):
        imp = imp + p[:, r * TQ:(r + 1) * TQ]

    blk = lax.broadcasted_iota(jnp.int32, (n_cmp, TQ), 0)
    tq1 = tq[:, 0:TQ]
    forced = (blk == _div_pow2(tq1, SEL_BLOCK)) | (blk == 0)
    visible = (blk * CMP_BLOCK + (CMP_BLOCK - 1)) <= tq1
    score = jnp.where(forced, jnp.inf, jnp.where(visible, imp, -jnp.inf))
    sel_ref[...] = _topk_bias(score, N_SEL, blk, 0)

    def sel_bias(kt):
        rows = [jnp.broadcast_to(sel_ref[pl.ds(kt * blocks_per_tile + i, 1), :], (SEL_BLOCK, TQ))
                for i in range(blocks_per_tile)]
        tile = jnp.concatenate(rows, axis=0)
        return jnp.concatenate([tile] * N_REP, axis=1)

    mblk = lax.broadcasted_iota(jnp.int32, (n_moba, wide), 0)
    gate = _dot(kmean_ref[...].astype(BF16), qm)
    mb_ref[...] = _topk_bias(jnp.where(mblk < qt, gate, -jnp.inf), MOBA_TOPK, mblk, 0)

    def tile_body(kt, carry):
        start = pl.multiple_of(kt * TQ, TQ)
        slc = _flash_t(carry[0], kaug_ref[0, pl.ds(start, TQ), :], qn_aug, vs_ref[0, :, pl.ds(start, TQ)],
                       sel_bias(kt), None)
        moba = _flash_t(carry[1], kaug_ref[2, pl.ds(start, TQ), :], qm_aug, vm_ref[0, :, pl.ds(start, TQ)],
                        mb_ref[pl.ds(kt, 1), :], None)
        return slc, moba

    slc, moba = lax.fori_loop(0, qt, tile_body, (init, init))
    d_start = pl.multiple_of(t0, TQ)
    _, l, acc = _flash_t(slc, kaug_ref[0, pl.ds(d_start, TQ), :], qn_aug, vs_ref[0, :, pl.ds(d_start, TQ)],
                         sel_bias(qt), causal)
    o_slc = acc / l
    _, l, acc = _flash_t(moba, kaug_ref[2, pl.ds(d_start, TQ), :], qm_aug, vm_ref[0, :, pl.ds(d_start, TQ)],
                         None, causal)
    o_moba = acc / l

    w_len = WINDOW + TQ
    w_start = pl.multiple_of(jnp.maximum(qt - 2, 0) * TQ, TQ)
    wpos = w_start + lax.broadcasted_iota(jnp.int32, (w_len, wide), 0)
    s = jnp.where((wpos <= tq) & (wpos > tq - WINDOW), _dot(kaug_ref[1, pl.ds(w_start, w_len), :], qn_aug), NEG_BIG)
    pw = jnp.exp(s - jnp.max(s, axis=0, keepdims=True))
    o_win = _dot(vw_ref[0, :, pl.ds(w_start, w_len)], pw.astype(BF16)) / jnp.sum(pw, axis=0, keepdims=True)

    o_nsa = gate_row(0) * o_cmp + gate_row(1) * o_slc + gate_row(2) * o_win
    for r in range(N_REP):
        on_ref[0, r * HEAD_DIM:(r + 1) * HEAD_DIM, :] = o_nsa[:, r * TQ:(r + 1) * TQ]
        om_ref[0, r * HEAD_DIM:(r + 1) * HEAD_DIM, :] = o_moba[:, r * TQ:(r + 1) * TQ]


def attn_prompt(slopes, q_fm, k_tm, kv_fm, k_cmp, v_cmp_t, gt_fm):
    b, _, seq = q_fm.shape
    assert MOBA_BLOCK == TQ and seq % TQ == 0 and seq >= WINDOW + TQ
    n_cmp = seq // CMP_BLOCK
    q_spec = lambda off: pl.BlockSpec((1, KV_W, TQ), lambda bi, g, qt: (bi, off + g, qt))
    k_spec = lambda w: pl.BlockSpec((1, 1, 1, seq, HEAD_DIM), lambda bi, g, qt: (bi, w, g, 0, 0))
    v_spec = lambda state: pl.BlockSpec((1, HEAD_DIM, seq), lambda bi, g, qt: (bi, state * N_KV + g, 0))
    o_spec = pl.BlockSpec((1, KV_W, TQ), lambda bi, g, qt: (bi, g, qt))
    o_shape = jax.ShapeDtypeStruct((b, Q_W, seq), F32)
    return pl.pallas_call(
        functools.partial(_attn_prompt_kernel, seq=seq),
        grid=(b, N_KV, seq // TQ),
        in_specs=[pl.BlockSpec(memory_space=pltpu.SMEM), q_spec(0), q_spec(N_KV),
                  k_spec(0), k_spec(1), k_spec(2), v_spec(3), v_spec(7), v_spec(5),
                  pl.BlockSpec((1, 1, n_cmp, HEAD_DIM), lambda bi, g, qt: (bi, g, 0, 0)),
                  pl.BlockSpec((1, 1, HEAD_DIM, n_cmp), lambda bi, g, qt: (bi, g, 0, 0)),
                  pl.BlockSpec((1, GATE_PAD, TQ), lambda bi, g, qt: (bi, g, qt))],
        out_specs=[o_spec, o_spec],
        out_shape=[o_shape, o_shape],
        scratch_shapes=[pltpu.VMEM((3, seq, 2 * HEAD_DIM), BF16), pltpu.VMEM((seq // MOBA_BLOCK, HEAD_DIM), F32),
                        pltpu.VMEM((n_cmp, TQ), F32), pltpu.VMEM((seq // MOBA_BLOCK, N_REP * TQ), F32)],
        compiler_params=_cparams("parallel", "parallel", "arbitrary"),
        name="attn_prompt",
    )(slopes, q_fm, q_fm, k_tm, k_tm, k_tm, kv_fm, kv_fm, kv_fm, k_cmp, v_cmp_t, gt_fm)


def _flash_update(carry, q, k, v, bias, feature_major):
    m, l, acc = carry
    s = (_dot(q, k) if feature_major else _dot_t(q, k)) + bias
    m_new = jnp.maximum(m, jnp.max(s, axis=-1, keepdims=True))
    alpha = jnp.exp(m - m_new)
    p = jnp.exp(s - m_new)
    l = alpha * l + jnp.sum(p, axis=-1, keepdims=True)
    pv = _dot_t(p.astype(BF16), v) if feature_major else _dot(p.astype(BF16), v)
    return m_new, l, alpha * acc + pv


def _diag_blocks(o, rows_per_group):
    return jnp.concatenate(
        [o[g * rows_per_group:(g + 1) * rows_per_group, g * HEAD_DIM:(g + 1) * HEAD_DIM] for g in range(N_KV)], axis=0)


def _page_spec(layer, n_pool, n_pages, pages_per_step, i):
    return pl.BlockSpec((1, KV_W, PAGE_SIZE),
                        lambda bi, s, pt: (layer * n_pool + pt[bi * n_pages + s * pages_per_step + i], 0, 0))


def _block_sum_kernel(pt_ref, *refs, pages_per_step):
    del pt_ref
    o_ref = refs[pages_per_step]
    step = pl.program_id(1)

    @pl.when(step == 0)
    def _():
        o_ref[...] = jnp.zeros(o_ref.shape, F32)

    lane = lax.broadcasted_iota(jnp.int32, (KV_W, o_ref.shape[2]), 1)
    acc = o_ref[0]
    for i in range(pages_per_step):
        blk = _div_pow2(step * pages_per_step + i, MOBA_BLOCK // PAGE_SIZE)
        acc = acc + jnp.where(lane == blk, jnp.sum(refs[i][0], axis=1, keepdims=True), 0.0)
    o_ref[0] = acc


def moba_block_sums(pool, layer, page_table, pages_per_step, n_lanes):
    b, n_pages = page_table.shape
    n_pool = pool.shape[0] // DEPTH
    assert n_pages % pages_per_step == 0 and n_pages * PAGE_SIZE // MOBA_BLOCK <= n_lanes
    return pl.pallas_call(
        functools.partial(_block_sum_kernel, pages_per_step=pages_per_step),
        grid_spec=pltpu.PrefetchScalarGridSpec(
            num_scalar_prefetch=1,
            grid=(b, n_pages // pages_per_step),
            in_specs=[_page_spec(layer, n_pool, n_pages, pages_per_step, i) for i in range(pages_per_step)],
            out_specs=pl.BlockSpec((1, KV_W, n_lanes), lambda bi, s, pt: (bi, 0, 0)),
        ),
        out_shape=jax.ShapeDtypeStruct((b, KV_W, n_lanes), F32),
        compiler_params=_cparams("parallel", "arbitrary"),
        name="moba_block_sums",
    )(page_table.reshape(-1), *([pool] * pages_per_step))


def _sample_select_kernel(qn_ref, qm_ref, kc_ref, vc_ref, kmean_ref, slope_ref, tpos_ref,
                          ocmp_ref, sbias_ref, mbias_ref, *, past, n_new):
    rows = N_HEADS * n_new
    rpg = N_REP * n_new
    n_cmp = kc_ref.shape[1]
    n_sel_lanes = sbias_ref.shape[2]
    n_moba_lanes = mbias_ref.shape[2]
    slope = slope_ref[...]
    qpos = past + tpos_ref[...]

    blk = lax.broadcasted_iota(jnp.int32, (rows, n_cmp), 1)
    dist = qpos - (blk * CMP_BLOCK + (CMP_BLOCK - 1))
    p = _masked_softmax(_dot_t(qn_ref[0], kc_ref[0].astype(BF16)) - slope * dist.astype(F32), dist >= 0, -1)
    ocmp_ref[0] = _diag_blocks(_dot(p.astype(BF16), vc_ref[0].astype(BF16)), rpg)

    imp = jnp.sum(p.reshape(N_KV, N_REP, n_new, n_cmp), axis=1).reshape(N_KV * n_new, n_cmp)
    imp = jnp.concatenate([imp, jnp.zeros((N_KV * n_new, n_sel_lanes - n_cmp), F32)], axis=1)
    sblk = lax.broadcasted_iota(jnp.int32, (N_KV * n_new, n_sel_lanes), 1)
    assert n_new & (n_new - 1) == 0
    spos = past + (lax.broadcasted_iota(jnp.int32, (N_KV * n_new, 1), 0) & (n_new - 1))
    n_sel_blocks = -(-(past + n_new) // SEL_BLOCK)
    visible = (sblk * SEL_BLOCK + (SEL_BLOCK - 1)) <= spos
    forced = (sblk == _div_pow2(spos, SEL_BLOCK)) | (sblk == 0)
    score = jnp.where(forced, jnp.inf, jnp.where(visible, imp, -jnp.inf))
    score = jnp.where(sblk < n_sel_blocks, score, -jnp.inf)
    sb = _topk_bias(score, N_SEL, sblk, -1)
    sb = jnp.broadcast_to(sb.reshape(N_KV, 1, n_new, n_sel_lanes), (N_KV, N_REP, n_new, n_sel_lanes))
    sbias_ref[0] = sb.reshape(rows, n_sel_lanes)

    mblk = lax.broadcasted_iota(jnp.int32, (rows, n_moba_lanes), 1)
    own = _div_pow2(qpos, MOBA_BLOCK)
    gate = _dot(qm_ref[0], kmean_ref[0].astype(BF16))
    mscore = jnp.where(mblk == own, jnp.inf, jnp.where(mblk < own, gate, -jnp.inf))
    mbias_ref[0] = _topk_bias(mscore, MOBA_TOPK + 1, mblk, -1)


def sample_select(qn, qm, kc, vc, kmean_t, slope, tpos, past, n_new, n_sel_lanes):
    b, rows, _ = qn.shape

    def spec(a):
        return pl.BlockSpec((1,) + a.shape[1:], lambda bi: (bi,) + (0,) * (a.ndim - 1))

    def full(a):
        return pl.BlockSpec(a.shape, lambda bi: (0,) * a.ndim)

    outs = [jax.ShapeDtypeStruct((b, rows, HEAD_DIM), F32),
            jax.ShapeDtypeStruct((b, rows, n_sel_lanes), F32),
            jax.ShapeDtypeStruct((b, rows, kmean_t.shape[2]), F32)]
    return pl.pallas_call(
        functools.partial(_sample_select_kernel, past=past, n_new=n_new),
        grid=(b,),
        in_specs=[spec(qn), spec(qm), spec(kc), spec(vc), spec(kmean_t), full(slope), full(tpos)],
        out_specs=[spec(o) for o in outs],
        out_shape=outs,
        compiler_params=_cparams("parallel"),
        name="sample_select",
    )(qn, qm, kc, vc, kmean_t, slope, tpos)


def _attn_sample_kernel(pt_ref, *refs, pages_per_step, past, n_new):
    del pt_ref
    pps = pages_per_step
    (qn_ref, qm_ref, sbias_ref, mbias_ref, slope_ref, tpos_ref) = refs[:6]
    page_refs = refs[6:6 + 4 * pps]
    (ksn_ref, vsn_ref, kmn_ref, vmn_ref, kw_ref, vw_ref, ocmp_ref, gt_ref) = refs[6 + 4 * pps:14 + 4 * pps]
    on_ref, om_ref = refs[14 + 4 * pps:16 + 4 * pps]
    ms_ref, ls_ref, as_ref, mm_ref, lm_ref, am_ref = refs[16 + 4 * pps:]
    step = pl.program_id(1)
    rpg = N_REP * n_new
    qn = qn_ref[0]
    qm = qm_ref[0]
    slope = slope_ref[...]
    tpos = tpos_ref[...]

    @pl.when(step == 0)
    def _():
        ms_ref[...] = jnp.full(ms_ref.shape, NEG_BIG, F32)
        mm_ref[...] = jnp.full(mm_ref.shape, NEG_BIG, F32)
        ls_ref[...] = jnp.zeros(ls_ref.shape, F32)
        lm_ref[...] = jnp.zeros(lm_ref.shape, F32)
        as_ref[...] = jnp.zeros(as_ref.shape, F32)
        am_ref[...] = jnp.zeros(am_ref.shape, F32)

    sbias = sbias_ref[0].astype(BF16)
    mbias = mbias_ref[0].astype(BF16)
    key = lax.broadcasted_iota(jnp.int32, (1, PAGE_SIZE), 1)
    s_n = lax.broadcasted_iota(jnp.int32, (sbias.shape[1], PAGE_SIZE), 0)
    s_j = _div_pow2(lax.broadcasted_iota(jnp.int32, (sbias.shape[1], PAGE_SIZE), 1), SEL_BLOCK)
    m_n = lax.broadcasted_iota(jnp.int32, (mbias.shape[1], PAGE_SIZE), 0)
    slc = (ms_ref[...], ls_ref[...], as_ref[...])
    moba = (mm_ref[...], lm_ref[...], am_ref[...])
    for i in range(pps):
        page = step * pps + i
        alibi = slope * (page * PAGE_SIZE - past + key).astype(F32)
        s_expand = (s_n == page * (PAGE_SIZE // SEL_BLOCK) + s_j).astype(BF16)
        m_expand = (m_n == _div_pow2(page, MOBA_BLOCK // PAGE_SIZE)).astype(BF16)
        slc = _flash_update(slc, qn, page_refs[i][0].astype(BF16), page_refs[pps + i][0].astype(BF16),
                            alibi + _dot(sbias, s_expand), True)
        moba = _flash_update(moba, qm, page_refs[2 * pps + i][0].astype(BF16), page_refs[3 * pps + i][0].astype(BF16),
                             alibi + _dot(mbias, m_expand), True)
    ms_ref[...], ls_ref[...], as_ref[...] = slc
    mm_ref[...], lm_ref[...], am_ref[...] = moba

    @pl.when(step == pl.num_programs(1) - 1)
    def _():
        new = lax.broadcasted_iota(jnp.int32, (1, n_new), 1)
        bias = jnp.where(new <= tpos, slope * new.astype(F32), NEG_BIG)
        _, l, acc = _flash_update(slc, qn, ksn_ref[0].astype(BF16), vsn_ref[0].astype(BF16), bias, False)
        o_slc = _diag_blocks(acc / l, rpg)
        _, l, acc = _flash_update(moba, qm, kmn_ref[0].astype(BF16), vmn_ref[0].astype(BF16), bias, False)
        om_ref[0] = _diag_blocks(acc / l, rpg)

        n_ctx = kw_ref.shape[2]
        wrel = lax.broadcasted_iota(jnp.int32, (1, n_ctx), 1) - (n_ctx - n_new)
        dist = tpos - wrel
        valid = (dist >= 0) & (dist < WINDOW) & (wrel + past >= 0)
        s = jnp.where(valid, _dot(qn, kw_ref[0].astype(BF16)) + slope * wrel.astype(F32), NEG_BIG)
        pw = jnp.exp(s - jnp.max(s, axis=-1, keepdims=True))
        o_win = _diag_blocks(_dot_t(pw.astype(BF16), vw_ref[0].astype(BF16)) / jnp.sum(pw, axis=-1, keepdims=True), rpg)

        gates = jax.nn.sigmoid(gt_ref[0])
        on_ref[0] = gates[:, 0:1] * ocmp_ref[0] + gates[:, 1:2] * o_slc + gates[:, 2:3] * o_win


def attn_sample(page_table, layer, qn, qm, sbias, mbias, slope, tpos, pools, new_kv, kwin_t, vwin_t, ocmp, gt,
                past, n_new, pages_per_step):
    b, rows, _ = qn.shape
    n_pages = page_table.shape[1]
    n_pool = pools[0].shape[0] // DEPTH
    pps = pages_per_step
    assert n_pages % pps == 0 and n_pages * PAGE_SIZE == past

    def spec(a):
        return pl.BlockSpec((1,) + a.shape[1:], lambda bi, s, pt: (bi,) + (0,) * (a.ndim - 1))

    def full(a):
        return pl.BlockSpec(a.shape, lambda bi, s, pt: (0,) * a.ndim)

    page_specs = [_page_spec(layer, n_pool, n_pages, pps, i) for _ in range(4) for i in range(pps)]
    page_args = [pool for pool in pools for _ in range(pps)]
    o_shape = jax.ShapeDtypeStruct((b, rows, HEAD_DIM), F32)
    return pl.pallas_call(
        functools.partial(_attn_sample_kernel, pages_per_step=pps, past=past, n_new=n_new),
        grid_spec=pltpu.PrefetchScalarGridSpec(
            num_scalar_prefetch=1,
            grid=(b, n_pages // pps),
            in_specs=[spec(qn), spec(qm), spec(sbias), spec(mbias), full(slope), full(tpos)] + page_specs
                     + [spec(a) for a in new_kv] + [spec(kwin_t), spec(vwin_t), spec(ocmp), spec(gt)],
            out_specs=[spec(o_shape), spec(o_shape)],
            scratch_shapes=[pltpu.VMEM((rows, 1), F32), pltpu.VMEM((rows, 1), F32), pltpu.VMEM((rows, KV_W), F32),
                            pltpu.VMEM((rows, 1), F32), pltpu.VMEM((rows, 1), F32), pltpu.VMEM((rows, KV_W), F32)],
        ),
        out_shape=[o_shape, o_shape],
        compiler_params=_cparams("parallel", "arbitrary"),
        name="attn_sample",
    )(page_table.reshape(-1), qn, qm, sbias, mbias, slope, tpos, *page_args, *new_kv, kwin_t, vwin_t, ocmp, gt)


def _out_proj(o_nsa, o_moba, gn_ref, gm_ref, wn_ref, wm_ref, x_ref, g_ref, b_ref, o_ref):
    hn = _rms_norm(o_nsa, gn_ref[...]).astype(BF16)
    hm = _rms_norm(o_moba, gm_ref[...]).astype(BF16)
    y = _dot(hn, wn_ref[...]) + _dot(hm, wm_ref[...])
    o_ref[...] = _layer_norm(DEEPNORM_ALPHA * x_ref[...] + y, g_ref[...], b_ref[...])


def _out_proj_kernel(on_ref, om_ref, *refs):
    _out_proj(on_ref[...], om_ref[...], *refs)


def _out_proj_fm_kernel(on_ref, om_ref, *refs):
    _out_proj(on_ref[0].T, om_ref[0].T, *refs)


def out_proj_ln(o_nsa, o_moba, gn_nsa, gn_moba, w_nsa, w_moba, x, g, b, feature_major):
    m = x.shape[0]
    assert m % ROW_TILE == 0
    row = lambda w: pl.BlockSpec((ROW_TILE, w), lambda i: (i, 0))
    full = lambda a: pl.BlockSpec(a.shape, lambda i: (0,) * a.ndim)
    if feature_major:
        per_batch = o_nsa.shape[2] // ROW_TILE
        o_spec = pl.BlockSpec((1, Q_W, ROW_TILE), lambda i: (i // per_batch, 0, i % per_batch))
    else:
        o_spec = row(Q_W)
    return pl.pallas_call(
        _out_proj_fm_kernel if feature_major else _out_proj_kernel,
        grid=(m // ROW_TILE,),
        in_specs=[o_spec, o_spec, full(gn_nsa), full(gn_moba), full(w_nsa), full(w_moba),
                  row(D_MODEL), full(g), full(b)],
        out_specs=row(D_MODEL),
        out_shape=jax.ShapeDtypeStruct((m, D_MODEL), F32),
        compiler_params=_cparams("parallel"),
        name="out_proj_ln",
    )(o_nsa, o_moba, gn_nsa, gn_moba, w_nsa, w_moba, x, g, b)


def _mem_attn_kernel(x_ref, wq_ref, k_ref, v_ref, wo_ref, g_ref, b_ref, o_ref):
    x = x_ref[0]
    q = _dot(x.astype(BF16), wq_ref[...]).astype(BF16)
    k = k_ref[0].astype(BF16)
    v = v_ref[0].astype(BF16)
    heads = []
    for h in range(MEM_HEADS):
        sl = slice(h * MEM_HEAD_DIM, (h + 1) * MEM_HEAD_DIM)
        s = _dot_t(q[:, sl], k[:, sl]) * (MEM_HEAD_DIM ** -0.5)
        e = jnp.exp(s - jnp.max(s, axis=-1, keepdims=True))
        p = e / jnp.sum(e, axis=-1, keepdims=True)
        heads.append(_dot(p.astype(BF16), v[:, sl]))
    o = jnp.concatenate(heads, axis=-1).astype(BF16)
    o_ref[0] = _layer_norm(DEEPNORM_ALPHA * x + _dot(o, wo_ref[...]), g_ref[...], b_ref[...])


def mem_attn_ln(x, wq, mem_k, mem_v, wo, g, b, tq):
    bsz, t, d = x.shape
    n_mem, d_mem = mem_k.shape[1:]
    assert t % tq == 0
    full = lambda a: pl.BlockSpec(a.shape, lambda bi, i: (0,) * a.ndim)
    return pl.pallas_call(
        _mem_attn_kernel,
        grid=(bsz, t // tq),
        in_specs=[pl.BlockSpec((1, tq, d), lambda bi, i: (bi, i, 0)), full(wq),
                  pl.BlockSpec((1, n_mem, d_mem), lambda bi, i: (bi, 0, 0)),
                  pl.BlockSpec((1, n_mem, d_mem), lambda bi, i: (bi, 0, 0)), full(wo), full(g), full(b)],
        out_specs=pl.BlockSpec((1, tq, d), lambda bi, i: (bi, i, 0)),
        out_shape=jax.ShapeDtypeStruct(x.shape, F32),
        compiler_params=_cparams("parallel", "arbitrary"),
        name="mem_attn_ln",
    )(x, wq, mem_k, mem_v, wo, g, b)


def _router_kernel(x_ref, w_ref, b_ref, e_ref, g_ref):
    logits = jnp.dot(x_ref[...], w_ref[...], preferred_element_type=F32, precision=lax.Precision.HIGHEST) + b_ref[...]
    lane = lax.broadcasted_iota(jnp.int32, logits.shape, 1).astype(F32)
    cur = logits
    vals, idxs = [], []
    for _ in range(TOP_K):
        m = jnp.max(cur, axis=-1, keepdims=True)
        idx = jnp.min(jnp.where(cur == m, lane, float(N_EXPERTS)), axis=-1, keepdims=True)
        vals.append(m)
        idxs.append(idx)
        cur = jnp.where(lane == idx, -jnp.inf, cur)
    top_v = jnp.concatenate(vals, axis=-1)
    e = jnp.exp(top_v - vals[0])
    g_ref[...] = e / jnp.sum(e, axis=-1, keepdims=True)
    e_ref[...] = jnp.concatenate(idxs, axis=-1).astype(jnp.int32)


def router(x, w, b):
    m, d = x.shape
    assert m % ROW_TILE == 0
    return pl.pallas_call(
        _router_kernel,
        grid=(m // ROW_TILE,),
        in_specs=[pl.BlockSpec((ROW_TILE, d), lambda i: (i, 0)),
                  pl.BlockSpec(w.shape, lambda i: (0, 0)), pl.BlockSpec(b.shape, lambda i: (0, 0))],
        out_specs=[pl.BlockSpec((ROW_TILE, TOP_K), lambda i: (i, 0)), pl.BlockSpec((ROW_TILE, TOP_K), lambda i: (i, 0))],
        out_shape=[jax.ShapeDtypeStruct((m, TOP_K), jnp.int32), jax.ShapeDtypeStruct((m, TOP_K), F32)],
        compiler_params=_cparams("parallel"),
        name="router",
    )(x, w, b)


def _expert_up_kernel(be_ref, nu_ref, x_ref, wg_ref, wu_ref, bg_ref, bu_ref, h_ref):
    del be_ref

    @pl.when(pl.program_id(1) < nu_ref[0])
    def _():
        x = x_ref[...]
        g = _dot(x, wg_ref[0].astype(BF16)) + bg_ref[0]
        u = _dot(x, wu_ref[0].astype(BF16)) + bu_ref[0]
        g = jnp.minimum(g, SWIGLU_LIMIT)
        u = jnp.clip(u, -SWIGLU_LIMIT, SWIGLU_LIMIT)
        h_ref[...] = (g * jax.nn.sigmoid(SWIGLU_ALPHA * g) * (u + 1.0)).astype(BF16)

    @pl.when(pl.program_id(1) >= nu_ref[0])
    def _():
        h_ref[...] = jnp.zeros(h_ref.shape, BF16)


def _expert_down_kernel(be_ref, nu_ref, h_ref, wd_ref, bd_ref, y_ref):
    del be_ref

    @pl.when(pl.program_id(1) < nu_ref[0])
    def _():
        y_ref[...] = _dot(h_ref[...], wd_ref[0].astype(BF16)) + bd_ref[0]

    @pl.when(pl.program_id(1) >= nu_ref[0])
    def _():
        y_ref[...] = jnp.zeros(y_ref.shape, F32)


def expert_ffn(xs, blk_e, n_used, layer, wg, bg, wu, bu, wd, bd):
    cap, d = xs.shape
    n_blocks = cap // MOE_TILE
    d_ff = wg.shape[-1]
    we = lambda j, i, be, nu: layer * N_EXPERTS + be[i]
    h = pl.pallas_call(
        _expert_up_kernel,
        grid_spec=pltpu.PrefetchScalarGridSpec(
            num_scalar_prefetch=2,
            grid=(d_ff // FF_CHUNK, n_blocks),
            in_specs=[pl.BlockSpec((MOE_TILE, d), lambda j, i, be, nu: (i, 0)),
                      pl.BlockSpec((1, d, FF_CHUNK), lambda j, i, be, nu: (we(j, i, be, nu), 0, j)),
                      pl.BlockSpec((1, d, FF_CHUNK), lambda j, i, be, nu: (we(j, i, be, nu), 0, j)),
                      pl.BlockSpec((1, 1, FF_CHUNK), lambda j, i, be, nu: (we(j, i, be, nu), 0, j)),
                      pl.BlockSpec((1, 1, FF_CHUNK), lambda j, i, be, nu: (we(j, i, be, nu), 0, j))],
            out_specs=pl.BlockSpec((MOE_TILE, FF_CHUNK), lambda j, i, be, nu: (i, j)),
        ),
        out_shape=jax.ShapeDtypeStruct((cap, d_ff), BF16),
        compiler_params=_cparams("parallel", "arbitrary"),
        name="expert_up",
    )(blk_e, n_used, xs, wg, wu, bg, bu)
    return pl.pallas_call(
        _expert_down_kernel,
        grid_spec=pltpu.PrefetchScalarGridSpec(
            num_scalar_prefetch=2,
            grid=(d // FF_CHUNK, n_blocks),
            in_specs=[pl.BlockSpec((MOE_TILE, d_ff), lambda j, i, be, nu: (i, 0)),
                      pl.BlockSpec((1, d_ff, FF_CHUNK), lambda j, i, be, nu: (we(j, i, be, nu), 0, j)),
                      pl.BlockSpec((1, 1, FF_CHUNK), lambda j, i, be, nu: (we(j, i, be, nu), 0, j))],
            out_specs=pl.BlockSpec((MOE_TILE, FF_CHUNK), lambda j, i, be, nu: (i, j)),
        ),
        out_shape=jax.ShapeDtypeStruct((cap, d), F32),
        compiler_params=_cparams("parallel", "arbitrary"),
        name="expert_down",
    )(blk_e, n_used, h, wd, bd)


def _residual_ln_kernel(x_ref, m_ref, g_ref, b_ref, o_ref):
    o_ref[...] = _layer_norm(DEEPNORM_ALPHA * x_ref[...] + m_ref[...], g_ref[...], b_ref[...])


def residual_ln(x, m, g, b):
    n, d = x.shape
    row = pl.BlockSpec((ROW_TILE, d), lambda i: (i, 0))
    full = lambda a: pl.BlockSpec(a.shape, lambda i: (0,) * a.ndim)
    return pl.pallas_call(
        _residual_ln_kernel,
        grid=(n // ROW_TILE,),
        in_specs=[row, row, full(g), full(b)],
        out_specs=row,
        out_shape=jax.ShapeDtypeStruct((n, d), F32),
        compiler_params=_cparams("parallel"),
        name="residual_ln",
    )(x, m, g, b)


def moe_ffn(x, layer, router_w, router_b, wg, bg, wu, bu, wd, bd):
    n, d = x.shape
    top_e, gate = router(x, router_w, router_b)
    a = n * TOP_K
    e_flat = top_e.reshape(a)
    onehot = (e_flat[:, None] == jnp.arange(N_EXPERTS, dtype=jnp.int32)[None, :]).astype(jnp.int32)
    csum = jnp.cumsum(onehot, axis=0)
    rank = jnp.sum((csum - onehot) * onehot, axis=1)
    counts = csum[-1]
    padded = (counts + MOE_TILE - 1) // MOE_TILE * MOE_TILE
    pad_end = jnp.cumsum(padded)
    pad_start = pad_end - padded
    dest = pad_start[e_flat] + rank
    n_blocks = -(-a // MOE_TILE) + N_EXPERTS
    cap = n_blocks * MOE_TILE
    tok = jnp.arange(a, dtype=jnp.int32) // TOP_K
    src = jnp.full((cap,), n, jnp.int32).at[dest].set(tok)
    blk_start = jnp.arange(n_blocks, dtype=jnp.int32) * MOE_TILE
    blk_e = jnp.minimum(jnp.sum(blk_start[:, None] >= pad_end[None, :], axis=1), N_EXPERTS - 1).astype(jnp.int32)
    n_used = (pad_end[-1] // MOE_TILE).astype(jnp.int32).reshape(1)
    x_pad = jnp.concatenate([x.astype(BF16), jnp.zeros((1, d), BF16)], axis=0)
    xs = x_pad[src]
    y = expert_ffn(xs, blk_e, n_used, layer, wg, bg, wu, bu, wd, bd)
    return jnp.sum(y[dest].reshape(n, TOP_K, d) * gate[:, :, None], axis=1)


def _alibi_slopes():
    return jnp.asarray(2.0 ** (-8.0 * np.arange(1, N_HEADS + 1) / N_HEADS), F32)


def _split_w_in(w):
    q_n, kv_n, gt, q_m, kv_m = jnp.split(w, [Q_W, Q_W + 6 * KV_W, Q_W + 6 * KV_W + 3 * N_HEADS,
                                            2 * Q_W + 6 * KV_W + 3 * N_HEADS], axis=1)
    kc, vc, ks, vs, kw, vw = jnp.split(kv_n, 6, axis=1)
    km, vm = jnp.split(kv_m, 2, axis=1)
    return q_n, (kc, vc, ks, vs, kw, vw, km, vm), gt, q_m


def _block_diag_rows(q, b, t):
    q = (q * (HEAD_DIM ** -0.5)).astype(BF16).reshape(b, t, N_KV, N_REP, HEAD_DIM).transpose(0, 2, 3, 1, 4)
    eye = jnp.eye(N_KV, dtype=BF16)
    bd = q[:, :, :, :, None, :] * eye[None, :, None, None, :, None]
    return bd.reshape(b, N_HEADS * t, KV_W)


def _rows_to_tokens(o, b, t):
    return o.reshape(b, N_KV, N_REP, t, HEAD_DIM).transpose(0, 3, 1, 2, 4).reshape(b * t, Q_W)


def _token_minor(a):
    lead = a.shape[:-2]
    a = a.reshape(lead + (N_KV, HEAD_DIM, a.shape[-1]))
    n = len(lead)
    return a.transpose(tuple(range(n)) + (n + 2, n, n + 1))


def kernel(x_prompt, x_sample, cache_nsa_cmp_k, cache_nsa_cmp_v, cache_nsa_slc_k, cache_nsa_slc_v, cache_moba_k, cache_moba_v, cache_nsa_win_k, cache_nsa_win_v, cache_mem_k, cache_mem_v, page_table, mem_prompt, w_in, nsa_cmp_pos, nsa_cmp_k_w1, nsa_cmp_k_w2, nsa_cmp_v_w1, nsa_cmp_v_w2, gn_nsa, gn_moba, w_out, ln1_g, ln1_b, ca_wq, ca_wk, ca_wv, ca_wo, ln2_g, ln2_b, router_w, router_b, exp_wg, exp_bg, exp_wu, exp_bu, exp_wd, exp_bd, ln3_g, ln3_b):
    bp, tp, d = x_prompt.shape
    bs, ts, _ = x_sample.shape
    n_p, n_s = bp * tp, bs * ts
    n_mem = mem_prompt.shape[1]
    depth, n_pool = cache_nsa_cmp_k.shape[:2]
    n_pages = page_table.shape[1]
    past = n_pages * PAGE_SIZE
    wbuf = cache_nsa_win_k.shape[2]
    d_mem = MEM_HEADS * MEM_HEAD_DIM
    slopes = _alibi_slopes()
    fm = lambda c: c.transpose(0, 1, 3, 4, 2).reshape(c.shape[0], c.shape[1], KV_W, c.shape[2])
    pool = lambda c: fm(c).reshape(depth * n_pool, KV_W, PAGE_SIZE)
    pools = tuple(map(pool, (cache_nsa_slc_k, cache_nsa_slc_v, cache_moba_k, cache_moba_v)))
    win_k_fm, win_v_fm = fm(cache_nsa_win_k), fm(cache_nsa_win_v)
    wg = exp_wg.reshape((depth * N_EXPERTS,) + exp_wg.shape[2:])
    wu = exp_wu.reshape((depth * N_EXPERTS,) + exp_wu.shape[2:])
    wd = exp_wd.reshape((depth * N_EXPERTS,) + exp_wd.shape[2:])
    bg = exp_bg.reshape(depth * N_EXPERTS, 1, -1)
    bu = exp_bu.reshape(depth * N_EXPERTS, 1, -1)
    bd = exp_bd.reshape(depth * N_EXPERTS, 1, -1)

    row = np.arange(N_HEADS * ts)
    s_slope = slopes[row // ts].reshape(-1, 1)
    s_tpos = jnp.asarray((row % ts).reshape(-1, 1), jnp.int32)
    n_cmp_s = (past + ts) // CMP_BLOCK
    n_sel_lanes = -(-(-(-(past + ts) // SEL_BLOCK)) // 128) * 128
    n_moba_past = past // MOBA_BLOCK

    x_p = x_prompt
    x_s = x_sample.reshape(n_s, d)
    p_states, s_states, mem_states = [], [], []
    s_win = []
    for l in range(depth):
        q_n, kv, gt, q_m = _split_w_in(w_in[l])
        kc, vc, ks, vs, kw, vw, km, vm = kv
        pos = nsa_cmp_pos[l].reshape(1, CMP_BLOCK * HEAD_DIM)
        ck_w1, ck_w2 = nsa_cmp_k_w1[l].astype(BF16), nsa_cmp_k_w2[l].astype(BF16)
        cv_w1, cv_w2 = nsa_cmp_v_w1[l].astype(BF16), nsa_cmp_v_w2[l].astype(BF16)

        wq_t = jnp.concatenate([q_n, q_m], axis=1).T.astype(BF16)
        wkv_t = jnp.concatenate([kc, vc, ks, vs, km, vm, kw, vw], axis=1).T.astype(BF16)
        wgt_t = jnp.pad(gt.reshape(d, N_KV, 3 * N_REP), ((0, 0), (0, 0), (0, GATE_PAD - 3 * N_REP)))
        wgt_t = wgt_t.reshape(d, N_KV * GATE_PAD).T.astype(BF16)
        wk_tm = jnp.concatenate([kc, vc, ks, kw, km], axis=1).astype(BF16)
        (q_fm,) = proj_feature_major(x_p, wq_t, (BF16,), HEAD_DIM ** -0.5)
        st_fm, kv_fm = proj_feature_major(x_p, wkv_t, (F32, BF16))
        (gt_fm,) = proj_feature_major(x_p, wgt_t, (F32,))
        kc_tm, vc_tm, k_tm = proj_keys_token_major(x_p, wk_tm)
        n_cmp_p = tp // CMP_BLOCK
        k_cmp = cmp_mlp(kc_tm.reshape(bp * N_KV * n_cmp_p, CMP_BLOCK * HEAD_DIM), pos, ck_w1, ck_w2, 256)
        v_cmp = cmp_mlp(vc_tm.reshape(bp * N_KV * n_cmp_p, CMP_BLOCK * HEAD_DIM), pos, cv_w1, cv_w2, 256)
        k_cmp = k_cmp.reshape(bp, N_KV, n_cmp_p, HEAD_DIM)
        v_cmp_t = v_cmp.reshape(bp, N_KV, n_cmp_p, HEAD_DIM).transpose(0, 1, 3, 2)
        o_nsa_p, o_moba_p = attn_prompt(slopes, q_fm, k_tm, kv_fm, k_cmp, v_cmp_t, gt_fm)

        w_main = jnp.concatenate([q_n, kc, vc, ks, vs, kw, vw, q_m, km, vm], axis=1).astype(BF16)
        w_gate = jnp.pad(gt, ((0, 0), (0, 128 - 3 * N_HEADS))).astype(BF16)
        proj_s = matmul(x_s, w_main, 256, 512)
        gt_s = matmul(x_s, w_gate, 256, 128)[:, :3 * N_HEADS]
        qs_n, kc_s, vc_s, ks_s, vs_s, kw_s, vw_s, qs_m, km_s, vm_s = jnp.split(
            proj_s, [Q_W + KV_W * i for i in range(7)] + [2 * Q_W + 6 * KV_W, 2 * Q_W + 7 * KV_W], axis=1)
        sn = lambda a: a.reshape(bs, ts, KV_W)

        def cmp_rows(cache):
            g = cache[l][page_table].reshape(bs, n_pages * PAGE_SIZE // CMP_BLOCK, CMP_BLOCK, N_KV, HEAD_DIM)
            return g.transpose(0, 1, 3, 2, 4).reshape(bs * n_cmp_s * N_KV, CMP_BLOCK * HEAD_DIM)

        k_cmp_s = cmp_mlp(cmp_rows(cache_nsa_cmp_k), pos, ck_w1, ck_w2, 256).reshape(bs, n_cmp_s, KV_W)
        v_cmp_s = cmp_mlp(cmp_rows(cache_nsa_cmp_v), pos, cv_w1, cv_w2, 256).reshape(bs, n_cmp_s, KV_W)
        k_mean_t = moba_block_sums(pools[2], l, page_table, 8, 128)
        k_mean_t = k_mean_t.at[:, :, n_moba_past].set(sn(km_s).sum(axis=1)) * (1.0 / MOBA_BLOCK)
        qn_s = _block_diag_rows(qs_n, bs, ts)
        qm_s = _block_diag_rows(qs_m, bs, ts)
        o_cmp_s, sbias, mbias = sample_select(qn_s, qm_s, k_cmp_s, v_cmp_s, k_mean_t, s_slope, s_tpos, past, ts, n_sel_lanes)
        kwin_t = jnp.concatenate([win_k_fm[l], sn(kw_s).transpose(0, 2, 1)], axis=2)
        vwin_t = jnp.concatenate([win_v_fm[l], sn(vw_s).transpose(0, 2, 1)], axis=2)
        gt_rows = gt_s.reshape(bs, ts, N_KV, N_REP, 3).transpose(0, 2, 3, 1, 4).reshape(bs, N_HEADS * ts, 3)
        o_nsa_s, o_moba_s = attn_sample(
            page_table, l, qn_s, qm_s, sbias, mbias, s_slope, s_tpos, pools, (sn(ks_s), sn(vs_s), sn(km_s), sn(vm_s)),
            kwin_t, vwin_t, o_cmp_s, gt_rows, past, ts, 8)

        w_o = w_out[l].astype(BF16)
        op_args = (gn_nsa[l].reshape(1, -1), gn_moba[l].reshape(1, -1), w_o[:Q_W], w_o[Q_W:])
        ln1 = (ln1_g[l].reshape(1, -1), ln1_b[l].reshape(1, -1))
        x_p = out_proj_ln(o_nsa_p, o_moba_p, *op_args, x_p.reshape(n_p, d), *ln1, True)
        x_s = out_proj_ln(_rows_to_tokens(o_nsa_s, bs, ts), _rows_to_tokens(o_moba_s, bs, ts), *op_args, x_s, *ln1, False)

        mem_kv = matmul(mem_prompt.reshape(bp * n_mem, d), jnp.concatenate([ca_wk[l], ca_wv[l]], axis=1).astype(BF16), 512, 512)
        mem_k, mem_v = mem_kv[:, :d_mem].reshape(bp, n_mem, d_mem), mem_kv[:, d_mem:].reshape(bp, n_mem, d_mem)
        wq, wo = ca_wq[l].astype(BF16), ca_wo[l].astype(BF16)
        g2, b2 = ln2_g[l].reshape(1, -1), ln2_b[l].reshape(1, -1)
        x_p = mem_attn_ln(x_p.reshape(bp, tp, d), wq, mem_k, mem_v, wo, g2, b2, ROW_TILE)
        x_s = mem_attn_ln(x_s.reshape(bs, ts, d), wq, cache_mem_k[l].reshape(bs, -1, d_mem),
                          cache_mem_v[l].reshape(bs, -1, d_mem), wo, g2, b2, ts)

        x = jnp.concatenate([x_p.reshape(n_p, d), x_s.reshape(n_s, d)], axis=0)
        m = moe_ffn(x, l, router_w[l], router_b[l].reshape(1, -1), wg, bg, wu, bu, wd, bd)
        x = residual_ln(x, m, ln3_g[l].reshape(1, -1), ln3_b[l].reshape(1, -1))
        x_p, x_s = x[:n_p].reshape(bp, tp, d), x[n_p:]

        p_states.append(st_fm)
        mem_states.append((mem_k.reshape(bp, n_mem, MEM_HEADS, MEM_HEAD_DIM), mem_v.reshape(bp, n_mem, MEM_HEADS, MEM_HEAD_DIM)))
        s_states.append([a.reshape(bs, ts, N_KV, HEAD_DIM) for a in (kc_s, vc_s, ks_s, vs_s, km_s, vm_s)])
        keep_s = min(WINDOW, wbuf + ts)
        s_win.append((kwin_t[:, :, -keep_s:], vwin_t[:, :, -keep_s:]))

    keep = min(WINDOW, tp)
    st = jnp.stack(p_states)
    p_out = [_token_minor(st[:, :, i * KV_W:(i + 1) * KV_W, :]) for i in range(6)]
    p_out += [_token_minor(st[:, :, i * KV_W:(i + 1) * KV_W, tp - keep:]) for i in (6, 7)]
    p_out += [jnp.stack([ms[i] for ms in mem_states]) for i in range(2)]
    s_out = [jnp.stack([ss[i] for ss in s_states]) for i in range(6)]
    s_out += [_token_minor(jnp.stack([w[i] for w in s_win])) for i in range(2)]
    return (x_p, x_s.reshape(bs, ts, d)) + tuple(p_out) + tuple(s_out)
```

```python
import functools

import numpy as np
import jax
import jax.numpy as jnp
from jax import lax
from jax.experimental import pallas as pl
from jax.experimental.pallas import tpu as pltpu

F32 = jnp.float32
BF16 = jnp.bfloat16

D_MODEL = 2048
HEAD_DIM = 64
N_KV = 4
N_REP = 4
N_HEADS = N_KV * N_REP
CMP_BLOCK = 64
SEL_BLOCK = 64
N_SEL = 8
WINDOW = 512
MOBA_BLOCK = 256
MOBA_TOPK = 3
MEM_HEADS = 4
MEM_HEAD_DIM = 128
N_EXPERTS = 32
TOP_K = 4
SWIGLU_LIMIT = 7.0
SWIGLU_ALPHA = 1.702
LN_EPS = 1e-5
NEG_BIG = -1e30
DEPTH = 2
DEEPNORM_ALPHA = (2 * DEPTH) ** 0.25
PAGE_SIZE = 128

KV_W = N_KV * HEAD_DIM
Q_W = N_HEADS * HEAD_DIM
N_STATE = 8
GATE_PAD = 16
ALIBI_PARTS = 3
TQ = 256
VMEM_LIMIT = 56 * 1024 * 1024
ROW_TILE = 256
PROJ_TILE = 512
MOE_TILE = 512
FF_CHUNK = 512
DOWN_CHUNK = 1024


def _cparams(*sem):
    return pltpu.CompilerParams(dimension_semantics=sem, vmem_limit_bytes=VMEM_LIMIT)


def _dot(a, b):
    return jnp.dot(a, b, preferred_element_type=F32)


def _dot_t(a, b):
    return lax.dot_general(a, b, (((1,), (1,)), ((), ())), preferred_element_type=F32)


def _div_pow2(x, n):
    assert n & (n - 1) == 0
    return lax.shift_right_logical(x, n.bit_length() - 1)


def _layer_norm(z, g, b):
    mu = jnp.mean(z, axis=-1, keepdims=True)
    zc = z - mu
    var = jnp.mean(zc * zc, axis=-1, keepdims=True)
    return zc * lax.rsqrt(var + LN_EPS) * g + b


def _rms_norm(x, g):
    return x * lax.rsqrt(jnp.mean(x * x, axis=-1, keepdims=True) + LN_EPS) * g


def _masked_softmax(s, mask, axis):
    s = jnp.where(mask, s, NEG_BIG)
    m = jnp.max(s, axis=axis, keepdims=True)
    e = jnp.where(mask, jnp.exp(s - m), 0.0)
    den = jnp.sum(e, axis=axis, keepdims=True)
    return e / jnp.where(den > 0.0, den, 1.0)


def _topk_bias(score, k, index, axis):
    index = index.astype(F32)
    cur = score
    bias = jnp.full(score.shape, NEG_BIG, F32)
    for _ in range(k):
        m = jnp.max(cur, axis=axis, keepdims=True)
        first = jnp.min(jnp.where(cur == m, index, float(score.shape[axis])), axis=axis, keepdims=True)
        pick = index == first
        bias = jnp.where(pick & (m > -jnp.inf), 0.0, bias)
        cur = jnp.where(pick, -jnp.inf, cur)
    return bias


def _mm_kernel(x_ref, w_ref, o_ref):
    o_ref[...] = _dot(x_ref[...].astype(BF16), w_ref[...])


def matmul(x, w, tm, tn):
    m, k = x.shape
    n = w.shape[1]
    tm = max(t for t in range(8, min(tm, m) + 1, 8) if m % t == 0)
    assert n % tn == 0
    return pl.pallas_call(
        _mm_kernel,
        grid=(m // tm, n // tn),
        in_specs=[pl.BlockSpec((tm, k), lambda i, j: (i, 0)),
                  pl.BlockSpec((k, tn), lambda i, j: (0, j))],
        out_specs=pl.BlockSpec((tm, tn), lambda i, j: (i, j)),
        out_shape=jax.ShapeDtypeStruct((m, n), F32),
        compiler_params=_cparams("parallel", "arbitrary"),
        name="matmul",
    )(x, w)


def _proj_fm_kernel(x_ref, w_ref, *o_refs, scale):
    r = _dot_t(w_ref[...], x_ref[0].astype(BF16))
    if scale != 1.0:
        r = r * scale
    for o_ref in o_refs:
        o_ref[0] = r.astype(o_ref.dtype)


def proj_feature_major(x, w_t, dtypes, scale=1.0):
    b, t, k = x.shape
    f = w_t.shape[0]
    tf = min(PROJ_TILE, f)
    assert t % PROJ_TILE == 0 and f % tf == 0
    o_spec = pl.BlockSpec((1, tf, PROJ_TILE), lambda bi, j, fi: (bi, fi, j))
    return pl.pallas_call(
        functools.partial(_proj_fm_kernel, scale=scale),
        grid=(b, t // PROJ_TILE, f // tf),
        in_specs=[pl.BlockSpec((1, PROJ_TILE, k), lambda bi, j, fi: (bi, j, 0)),
                  pl.BlockSpec((tf, k), lambda bi, j, fi: (fi, 0))],
        out_specs=[o_spec for _ in dtypes],
        out_shape=[jax.ShapeDtypeStruct((b, f, t), dt) for dt in dtypes],
        compiler_params=_cparams("parallel", "parallel", "arbitrary"),
        name="proj_feature_major",
    )(x, w_t)


def _proj_keys_kernel(x_ref, w_ref, kc_ref, vc_ref, k_ref):
    r = _dot(x_ref[0].astype(BF16), w_ref[...])
    for g in range(N_KV):
        lo = g * HEAD_DIM
        kc_ref[0, g] = r[:, lo:lo + HEAD_DIM]
        vc_ref[0, g] = r[:, KV_W + lo:KV_W + lo + HEAD_DIM]
        for w in range(3):
            k_ref[0, w, g] = r[:, (2 + w) * KV_W + lo:(2 + w) * KV_W + lo + HEAD_DIM].astype(BF16)


def proj_keys_token_major(x, w):
    b, t, k = x.shape
    assert t % PROJ_TILE == 0 and w.shape[1] == 5 * KV_W
    c_spec = pl.BlockSpec((1, N_KV, PROJ_TILE, HEAD_DIM), lambda bi, j: (bi, 0, j, 0))
    c_shape = jax.ShapeDtypeStruct((b, N_KV, t, HEAD_DIM), F32)
    return pl.pallas_call(
        _proj_keys_kernel,
        grid=(b, t // PROJ_TILE),
        in_specs=[pl.BlockSpec((1, PROJ_TILE, k), lambda bi, j: (bi, j, 0)),
                  pl.BlockSpec(w.shape, lambda bi, j: (0, 0))],
        out_specs=[c_spec, c_spec, pl.BlockSpec((1, 3, N_KV, PROJ_TILE, HEAD_DIM), lambda bi, j: (bi, 0, 0, j, 0))],
        out_shape=[c_shape, c_shape, jax.ShapeDtypeStruct((b, 3, N_KV, t, HEAD_DIM), BF16)],
        compiler_params=_cparams("parallel", "arbitrary"),
        name="proj_keys_token_major",
    )(x, w)


def _cmp_mlp_kernel(x_ref, pos_ref, w1_ref, w2_ref, o_ref):
    x = (x_ref[...] + pos_ref[...]).astype(BF16)
    h = _dot(x, w1_ref[...])
    h = h * jax.nn.sigmoid(h)
    o_ref[...] = _dot(h.astype(BF16), w2_ref[...])


def cmp_mlp(x, pos, w1, w2, tr):
    r, k = x.shape
    hid = w1.shape[1]
    tr = min(tr, r)
    assert r % tr == 0
    return pl.pallas_call(
        _cmp_mlp_kernel,
        grid=(r // tr,),
        in_specs=[pl.BlockSpec((tr, k), lambda i: (i, 0)),
                  pl.BlockSpec((1, k), lambda i: (0, 0)),
                  pl.BlockSpec((k, hid), lambda i: (0, 0)),
                  pl.BlockSpec((hid, HEAD_DIM), lambda i: (0, 0))],
        out_specs=pl.BlockSpec((tr, HEAD_DIM), lambda i: (i, 0)),
        out_shape=jax.ShapeDtypeStruct((r, HEAD_DIM), F32),
        compiler_params=_cparams("parallel"),
        name="cmp_mlp",
    )(x, pos, w1, w2)


def _flash_t(carry, k_tile, q_aug, vt_tile, bias, mask):
    m, l, acc = carry
    s = _dot(k_tile, q_aug)
    if bias is not None:
        s = s + bias
    if mask is not None:
        s = jnp.where(mask, s, NEG_BIG)
    m_new = jnp.maximum(m, jnp.max(s, axis=0, keepdims=True))
    alpha = jnp.exp(m - m_new)
    p = jnp.exp(s - m_new)
    l = alpha * l + jnp.sum(p, axis=0, keepdims=True)
    acc = alpha * acc + _dot(vt_tile, p.astype(BF16))
    return m_new, l, acc


def _attn_prompt_kernel(slopes_ref, qn_ref, qm_ref, ks_ref, kw_ref, km_ref, vs_ref, vw_ref, vm_ref,
                        kc_ref, vct_ref, gt_ref, on_ref, om_ref, kaug_ref, kmean_ref, sel_ref, mb_ref, *, seq):
    g = pl.program_id(1)
    qt = pl.program_id(2)
    n_cmp = seq // CMP_BLOCK
    n_moba = seq // MOBA_BLOCK
    blocks_per_tile = TQ // SEL_BLOCK
    t0 = qt * TQ

    @pl.when(qt == 0)
    def _():
        pos = lax.broadcasted_iota(jnp.int32, (seq, HEAD_DIM), 0).astype(F32)
        lane = lax.broadcasted_iota(jnp.int32, (seq, HEAD_DIM), 1)
        cols = jnp.zeros((seq, HEAD_DIM), F32)
        for r in range(N_REP):
            v = pos * slopes_ref[g * N_REP + r]
            hi = v.astype(BF16).astype(F32)
            mid = (v - hi).astype(BF16).astype(F32)
            lo = v - hi - mid
            c = ALIBI_PARTS * r
            cols = jnp.where(lane == c, hi, jnp.where(lane == c + 1, mid, jnp.where(lane == c + 2, lo, cols)))
        cols = cols.astype(BF16)
        for w, k_ref in enumerate((ks_ref, kw_ref, km_ref)):
            kaug_ref[w] = jnp.concatenate([k_ref[0, 0, 0], cols], axis=1)
        km = km_ref[0, 0, 0].astype(F32)
        kmean_ref[...] = jnp.mean(km.reshape(n_moba, MOBA_BLOCK, HEAD_DIM), axis=1)

    wide = N_REP * TQ
    per_rep = lambda f: jnp.concatenate([f(r) for r in range(N_REP)], axis=1)
    q_in_tile = lax.broadcasted_iota(jnp.int32, (1, wide), 1) & (TQ - 1)
    tq = t0 + q_in_tile
    causal = lax.broadcasted_iota(jnp.int32, (TQ, wide), 0) <= q_in_tile
    gates = jax.nn.sigmoid(gt_ref[0])
    gate_row = lambda branch: per_rep(lambda r: gates[3 * r + branch:3 * r + branch + 1, :])
    slope_row = per_rep(lambda r: jnp.full((1, TQ), slopes_ref[g * N_REP + r], F32))
    aug_row = lax.broadcasted_iota(jnp.int32, (HEAD_DIM, wide), 0)
    aug_lo = ALIBI_PARTS * _div_pow2(lax.broadcasted_iota(jnp.int32, (HEAD_DIM, wide), 1), TQ)
    ones = jnp.where((aug_row >= aug_lo) & (aug_row < aug_lo + ALIBI_PARTS), 1.0, 0.0).astype(BF16)
    qn = per_rep(lambda r: qn_ref[0, r * HEAD_DIM:(r + 1) * HEAD_DIM, :])
    qm = per_rep(lambda r: qm_ref[0, r * HEAD_DIM:(r + 1) * HEAD_DIM, :])
    qn_aug = jnp.concatenate([qn, ones], axis=0)
    qm_aug = jnp.concatenate([qm, ones], axis=0)
    init = (jnp.full((1, wide), NEG_BIG, F32), jnp.zeros((1, wide), F32), jnp.zeros((HEAD_DIM, wide), F32))

    kc = kc_ref[0, 0].astype(BF16)
    vct = vct_ref[0, 0].astype(BF16)
    k_end = lax.broadcasted_iota(jnp.int32, (n_cmp, wide), 0) * CMP_BLOCK + (CMP_BLOCK - 1)
    p = _masked_softmax(_dot(kc, qn) + slope_row * k_end.astype(F32), k_end <= tq, 0)
    o_cmp = _dot(vct, p.astype(BF16))
    imp = p[:, 0:TQ]
    for r in range(1, N_REP):
        imp = imp + p[:, r * TQ:(r + 1) * TQ]

    blk = lax.broadcasted_iota(jnp.int32, (n_cmp, TQ), 0)
    tq1 = tq[:, 0:TQ]
    forced = (blk == _div_pow2(tq1, SEL_BLOCK)) | (blk == 0)
    visible = (blk * CMP_BLOCK + (CMP_BLOCK - 1)) <= tq1
    score = jnp.where(forced, jnp.inf, jnp.where(visible, imp, -jnp.inf))
    sel_ref[...] = _topk_bias(score, N_SEL, blk, 0)

    def sel_bias(kt):
        rows = [jnp.broadcast_to(sel_ref[pl.ds(kt * blocks_per_tile + i, 1), :], (SEL_BLOCK, TQ))
                for i in range(blocks_per_tile)]
        tile = jnp.concatenate(rows, axis=0)
        return jnp.concatenate([tile] * N_REP, axis=1)

    mblk = lax.broadcasted_iota(jnp.int32, (n_moba, wide), 0)
    gate = _dot(kmean_ref[...].astype(BF16), qm)
    mb_ref[...] = _topk_bias(jnp.where(mblk < qt, gate, -jnp.inf), MOBA_TOPK, mblk, 0)

    def tile_body(kt, carry):
        start = pl.multiple_of(kt * TQ, TQ)
        slc = _flash_t(carry[0], kaug_ref[0, pl.ds(start, TQ), :], qn_aug, vs_ref[0, :, pl.ds(start, TQ)],
                       sel_bias(kt), None)
        moba = _flash_t(carry[1], kaug_ref[2, pl.ds(start, TQ), :], qm_aug, vm_ref[0, :, pl.ds(start, TQ)],
                        mb_ref[pl.ds(kt, 1), :], None)
        return slc, moba

    slc, moba = lax.fori_loop(0, qt, tile_body, (init, init))
    d_start = pl.multiple_of(t0, TQ)
    _, l, acc = _flash_t(slc, kaug_ref[0, pl.ds(d_start, TQ), :], qn_aug, vs_ref[0, :, pl.ds(d_start, TQ)],
                         sel_bias(qt), causal)
    o_slc = acc / l
    _, l, acc = _flash_t(moba, kaug_ref[2, pl.ds(d_start, TQ), :], qm_aug, vm_ref[0, :, pl.ds(d_start, TQ)],
                         None, causal)
    o_moba = acc / l

    w_len = WINDOW + TQ
    w_start = pl.multiple_of(jnp.maximum(qt - 2, 0) * TQ, TQ)
    wpos = w_start + lax.broadcasted_iota(jnp.int32, (w_len, wide), 0)
    s = jnp.where((wpos <= tq) & (wpos > tq - WINDOW), _dot(kaug_ref[1, pl.ds(w_start, w_len), :], qn_aug), NEG_BIG)
    pw = jnp.exp(s - jnp.max(s, axis=0, keepdims=True))
    o_win = _dot(vw_ref[0, :, pl.ds(w_start, w_len)], pw.astype(BF16)) / jnp.sum(pw, axis=0, keepdims=True)

    o_nsa = gate_row(0) * o_cmp + gate_row(1) * o_slc + gate_row(2) * o_win
    for r in range(N_REP):
        on_ref[0, r * HEAD_DIM:(r + 1) * HEAD_DIM, :] = o_nsa[:, r * TQ:(r + 1) * TQ]
        om_ref[0, r * HEAD_DIM:(r + 1) * HEAD_DIM, :] = o_moba[:, r * TQ:(r + 1) * TQ]


def attn_prompt(slopes, q_fm, k_tm, kv_fm, k_cmp, v_cmp_t, gt_fm):
    b, _, seq = q_fm.shape
    assert MOBA_BLOCK == TQ and seq % TQ == 0 and seq >= WINDOW + TQ
    n_cmp = seq // CMP_BLOCK
    q_spec = lambda off: pl.BlockSpec((1, KV_W, TQ), lambda bi, g, qt: (bi, off + g, qt))
    k_spec = lambda w: pl.BlockSpec((1, 1, 1, seq, HEAD_DIM), lambda bi, g, qt: (bi, w, g, 0, 0))
    v_spec = lambda state: pl.BlockSpec((1, HEAD_DIM, seq), lambda bi, g, qt: (bi, state * N_KV + g, 0))
    o_spec = pl.BlockSpec((1, KV_W, TQ), lambda bi, g, qt: (bi, g, qt))
    o_shape = jax.ShapeDtypeStruct((b, Q_W, seq), F32)
    return pl.pallas_call(
        functools.partial(_attn_prompt_kernel, seq=seq),
        grid=(b, N_KV, seq // TQ),
        in_specs=[pl.BlockSpec(memory_space=pltpu.SMEM), q_spec(0), q_spec(N_KV),
                  k_spec(0), k_spec(1), k_spec(2), v_spec(3), v_spec(7), v_spec(5),
                  pl.BlockSpec((1, 1, n_cmp, HEAD_DIM), lambda bi, g, qt: (bi, g, 0, 0)),
                  pl.BlockSpec((1, 1, HEAD_DIM, n_cmp), lambda bi, g, qt: (bi, g, 0, 0)),
                  pl.BlockSpec((1, GATE_PAD, TQ), lambda bi, g, qt: (bi, g, qt))],
        out_specs=[o_spec, o_spec],
        out_shape=[o_shape, o_shape],
        scratch_shapes=[pltpu.VMEM((3, seq, 2 * HEAD_DIM), BF16), pltpu.VMEM((seq // MOBA_BLOCK, HEAD_DIM), F32),
                        pltpu.VMEM((n_cmp, TQ), F32), pltpu.VMEM((seq // MOBA_BLOCK, N_REP * TQ), F32)],
        compiler_params=_cparams("parallel", "parallel", "arbitrary"),
        name="attn_prompt",
    )(slopes, q_fm, q_fm, k_tm, k_tm, k_tm, kv_fm, kv_fm, kv_fm, k_cmp, v_cmp_t, gt_fm)


def _flash_update(carry, q, k, v, bias):
    m, l, acc = carry
    s = _dot_t(q, k) + bias
    m_new = jnp.maximum(m, jnp.max(s, axis=-1, keepdims=True))
    alpha = jnp.exp(m - m_new)
    p = jnp.exp(s - m_new)
    l = alpha * l + jnp.sum(p, axis=-1, keepdims=True)
    return m_new, l, alpha * acc + _dot(p.astype(BF16), v)


def _diag_blocks(o, rows_per_group):
    return jnp.concatenate(
        [o[g * rows_per_group:(g + 1) * rows_per_group, g * HEAD_DIM:(g + 1) * HEAD_DIM] for g in range(N_KV)], axis=0)


def _page_spec(layer, n_pool, n_pages, pages_per_step, i):
    return pl.BlockSpec((1, KV_W, PAGE_SIZE),
                        lambda bi, s, pt: (layer * n_pool + pt[bi * n_pages + s * pages_per_step + i], 0, 0))


def _block_sum_kernel(pt_ref, *refs, pages_per_step):
    del pt_ref
    o_ref = refs[pages_per_step]
    step = pl.program_id(1)

    @pl.when(step == 0)
    def _():
        o_ref[...] = jnp.zeros(o_ref.shape, F32)

    lane = lax.broadcasted_iota(jnp.int32, (KV_W, o_ref.shape[2]), 1)
    acc = o_ref[0]
    for i in range(pages_per_step):
        blk = _div_pow2(step * pages_per_step + i, MOBA_BLOCK // PAGE_SIZE)
        acc = acc + jnp.where(lane == blk, jnp.sum(refs[i][0], axis=1, keepdims=True), 0.0)
    o_ref[0] = acc


def moba_block_sums(pool, layer, page_table, pages_per_step, n_lanes):
    b, n_pages = page_table.shape
    n_pool = pool.shape[0] // DEPTH
    assert n_pages % pages_per_step == 0 and n_pages * PAGE_SIZE // MOBA_BLOCK <= n_lanes
    return pl.pallas_call(
        functools.partial(_block_sum_kernel, pages_per_step=pages_per_step),
        grid_spec=pltpu.PrefetchScalarGridSpec(
            num_scalar_prefetch=1,
            grid=(b, n_pages // pages_per_step),
            in_specs=[_page_spec(layer, n_pool, n_pages, pages_per_step, i) for i in range(pages_per_step)],
            out_specs=pl.BlockSpec((1, KV_W, n_lanes), lambda bi, s, pt: (bi, 0, 0)),
        ),
        out_shape=jax.ShapeDtypeStruct((b, KV_W, n_lanes), F32),
        compiler_params=_cparams("parallel", "arbitrary"),
        name="moba_block_sums",
    )(page_table.reshape(-1), *([pool] * pages_per_step))


def _cmp_pages_kernel(pt_ref, *refs, n_pages):
    del pt_ref
    page_refs = refs[:n_pages]
    pos_ref, w1_ref, w2_ref, o_ref, buf_ref = refs[n_pages:]
    blocks_per_page = PAGE_SIZE // CMP_BLOCK
    hid = w2_ref.shape[0]
    for i in range(n_pages):
        buf_ref[i * KV_W:(i + 1) * KV_W, :] = page_refs[i][0]
    w2 = w2_ref[...]
    for g in range(N_KV):

        def body(d, acc, g=g):
            x = buf_ref[pl.ds(g * HEAD_DIM + d, n_pages, stride=KV_W), :]
            x = (x + pos_ref[pl.ds(d, 1), :]).astype(BF16)
            x = jnp.concatenate([x[:, i * CMP_BLOCK:(i + 1) * CMP_BLOCK] for i in range(blocks_per_page)], axis=0)
            return acc + _dot(x, w1_ref[d])

        h = lax.fori_loop(0, HEAD_DIM, body, jnp.zeros((blocks_per_page * n_pages, hid), F32), unroll=16)
        h = (h * jax.nn.sigmoid(h)).astype(BF16)
        y = _dot(h, w2)
        for blk in range(blocks_per_page):
            o_ref[0, blk, :, g * HEAD_DIM:(g + 1) * HEAD_DIM] = y[blk * n_pages:(blk + 1) * n_pages, :]


def cmp_pages(pool, layer, page_table, pos_t, w1_by_dim, w2):
    b, n_pages = page_table.shape
    n_pool = pool.shape[0] // DEPTH
    blocks_per_page = PAGE_SIZE // CMP_BLOCK
    page_spec = lambda i: pl.BlockSpec((1, KV_W, PAGE_SIZE), lambda bi, pt: (layer * n_pool + pt[bi * n_pages + i], 0, 0))
    full = lambda a: pl.BlockSpec(a.shape, lambda bi, pt: (0,) * a.ndim)
    return pl.pallas_call(
        functools.partial(_cmp_pages_kernel, n_pages=n_pages),
        grid_spec=pltpu.PrefetchScalarGridSpec(
            num_scalar_prefetch=1,
            grid=(b,),
            in_specs=[page_spec(i) for i in range(n_pages)] + [full(pos_t), full(w1_by_dim), full(w2)],
            out_specs=pl.BlockSpec((1, blocks_per_page, n_pages, KV_W), lambda bi, pt: (bi, 0, 0, 0)),
            scratch_shapes=[pltpu.VMEM((n_pages * KV_W, PAGE_SIZE), F32)],
        ),
        out_shape=jax.ShapeDtypeStruct((b, blocks_per_page, n_pages, KV_W), F32),
        compiler_params=_cparams("parallel"),
        name="cmp_pages",
    )(page_table.reshape(-1), *([pool] * n_pages), pos_t, w1_by_dim, w2)


def _cmp_w1_by_dim(w1):
    return w1.reshape(CMP_BLOCK, HEAD_DIM, w1.shape[1]).transpose(1, 0, 2).astype(BF16)


def _sample_select_kernel(qn_ref, qm_ref, kc_ref, vc_ref, kmean_ref, slope_ref, tpos_ref,
                          ocmp_ref, sbias_ref, mbias_ref, *, past, n_new):
    rows = N_HEADS * n_new
    rpg = N_REP * n_new
    n_cmp = kc_ref.shape[1]
    n_sel_lanes = sbias_ref.shape[2]
    n_moba_lanes = mbias_ref.shape[2]
    slope = slope_ref[...]
    qpos = past + tpos_ref[...]

    blk = lax.broadcasted_iota(jnp.int32, (rows, n_cmp), 1)
    dist = qpos - (blk * CMP_BLOCK + (CMP_BLOCK - 1))
    p = _masked_softmax(_dot_t(qn_ref[0], kc_ref[0].astype(BF16)) - slope * dist.astype(F32), dist >= 0, -1)
    ocmp_ref[0] = _diag_blocks(_dot(p.astype(BF16), vc_ref[0].astype(BF16)), rpg)

    imp = jnp.sum(p.reshape(N_KV, N_REP, n_new, n_cmp), axis=1).reshape(N_KV * n_new, n_cmp)
    imp = jnp.concatenate([imp, jnp.zeros((N_KV * n_new, n_sel_lanes - n_cmp), F32)], axis=1)
    sblk = lax.broadcasted_iota(jnp.int32, (N_KV * n_new, n_sel_lanes), 1)
    assert n_new & (n_new - 1) == 0
    spos = past + (lax.broadcasted_iota(jnp.int32, (N_KV * n_new, 1), 0) & (n_new - 1))
    n_sel_blocks = -(-(past + n_new) // SEL_BLOCK)
    visible = (sblk * SEL_BLOCK + (SEL_BLOCK - 1)) <= spos
    forced = (sblk == _div_pow2(spos, SEL_BLOCK)) | (sblk == 0)
    score = jnp.where(forced, jnp.inf, jnp.where(visible, imp, -jnp.inf))
    score = jnp.where(sblk < n_sel_blocks, score, -jnp.inf)
    sb = _topk_bias(score, N_SEL, sblk, -1)
    sb = jnp.broadcast_to(sb.reshape(N_KV, 1, n_new, n_sel_lanes), (N_KV, N_REP, n_new, n_sel_lanes))
    sbias_ref[0] = sb.reshape(rows, n_sel_lanes)

    mblk = lax.broadcasted_iota(jnp.int32, (rows, n_moba_lanes), 1)
    own = _div_pow2(qpos, MOBA_BLOCK)
    gate = _dot(qm_ref[0], kmean_ref[0].astype(BF16))
    mscore = jnp.where(mblk == own, jnp.inf, jnp.where(mblk < own, gate, -jnp.inf))
    mbias_ref[0] = _topk_bias(mscore, MOBA_TOPK + 1, mblk, -1)


def sample_select(qn, qm, kc, vc, kmean_t, slope, tpos, past, n_new, n_sel_lanes):
    b, rows, _ = qn.shape

    def spec(a):
        return pl.BlockSpec((1,) + a.shape[1:], lambda bi: (bi,) + (0,) * (a.ndim - 1))

    def full(a):
        return pl.BlockSpec(a.shape, lambda bi: (0,) * a.ndim)

    outs = [jax.ShapeDtypeStruct((b, rows, HEAD_DIM), F32),
            jax.ShapeDtypeStruct((b, rows, n_sel_lanes), F32),
            jax.ShapeDtypeStruct((b, rows, kmean_t.shape[2]), F32)]
    return pl.pallas_call(
        functools.partial(_sample_select_kernel, past=past, n_new=n_new),
        grid=(b,),
        in_specs=[spec(qn), spec(qm), spec(kc), spec(vc), spec(kmean_t), full(slope), full(tpos)],
        out_specs=[spec(o) for o in outs],
        out_shape=outs,
        compiler_params=_cparams("parallel"),
        name="sample_select",
    )(qn, qm, kc, vc, kmean_t, slope, tpos)


def _attn_sample_kernel(pt_ref, *refs, pages_per_step, past, n_new):
    del pt_ref
    pps = pages_per_step
    (qn_ref, qm_ref, sbias_ref, mbias_ref, slope_ref, tpos_ref) = refs[:6]
    page_refs = refs[6:6 + 4 * pps]
    (ksn_ref, vsn_ref, kmn_ref, vmn_ref, kw_ref, vw_ref, ocmp_ref, gt_ref) = refs[6 + 4 * pps:14 + 4 * pps]
    on_ref, om_ref = refs[14 + 4 * pps:16 + 4 * pps]
    ms_ref, ls_ref, as_ref, mm_ref, lm_ref, am_ref = refs[16 + 4 * pps:]
    step = pl.program_id(1)
    rpg = N_REP * n_new
    qn = qn_ref[0]
    qm = qm_ref[0]
    slope = slope_ref[...]
    tpos = tpos_ref[...]

    @pl.when(step == 0)
    def _():
        ms_ref[...] = jnp.full(ms_ref.shape, NEG_BIG, F32)
        mm_ref[...] = jnp.full(mm_ref.shape, NEG_BIG, F32)
        ls_ref[...] = jnp.zeros(ls_ref.shape, F32)
        lm_ref[...] = jnp.zeros(lm_ref.shape, F32)
        as_ref[...] = jnp.zeros(as_ref.shape, F32)
        am_ref[...] = jnp.zeros(am_ref.shape, F32)

    sbias = sbias_ref[0].astype(BF16)
    mbias = mbias_ref[0].astype(BF16)
    n_keys = pps * PAGE_SIZE
    first_key = step * n_keys
    key = lax.broadcasted_iota(jnp.int32, (1, n_keys), 1)
    alibi = slope * (first_key - past + key).astype(F32)
    s_blk = _div_pow2(first_key + lax.broadcasted_iota(jnp.int32, (sbias.shape[1], n_keys), 1), SEL_BLOCK)
    s_expand = (lax.broadcasted_iota(jnp.int32, (sbias.shape[1], n_keys), 0) == s_blk).astype(BF16)
    m_blk = _div_pow2(first_key + lax.broadcasted_iota(jnp.int32, (mbias.shape[1], n_keys), 1), MOBA_BLOCK)
    m_expand = (lax.broadcasted_iota(jnp.int32, (mbias.shape[1], n_keys), 0) == m_blk).astype(BF16)

    def update(carry, q, k_refs, v_refs, bias):
        m, l, acc = carry
        s = jnp.concatenate([_dot(q, k_ref[0].astype(BF16)) for k_ref in k_refs], axis=1) + bias
        m_new = jnp.maximum(m, jnp.max(s, axis=-1, keepdims=True))
        alpha = jnp.exp(m - m_new)
        p = jnp.exp(s - m_new)
        l = alpha * l + jnp.sum(p, axis=-1, keepdims=True)
        acc = alpha * acc
        for i, v_ref in enumerate(v_refs):
            acc = acc + _dot_t(p[:, i * PAGE_SIZE:(i + 1) * PAGE_SIZE].astype(BF16), v_ref[0].astype(BF16))
        return m_new, l, acc

    slc = update((ms_ref[...], ls_ref[...], as_ref[...]), qn, page_refs[:pps], page_refs[pps:2 * pps],
                 alibi + _dot(sbias, s_expand))
    moba = update((mm_ref[...], lm_ref[...], am_ref[...]), qm, page_refs[2 * pps:3 * pps], page_refs[3 * pps:],
                  alibi + _dot(mbias, m_expand))
    ms_ref[...], ls_ref[...], as_ref[...] = slc
    mm_ref[...], lm_ref[...], am_ref[...] = moba

    @pl.when(step == pl.num_programs(1) - 1)
    def _():
        new = lax.broadcasted_iota(jnp.int32, (1, n_new), 1)
        bias = jnp.where(new <= tpos, slope * new.astype(F32), NEG_BIG)
        _, l, acc = _flash_update(slc, qn, ksn_ref[0].astype(BF16), vsn_ref[0].astype(BF16), bias)
        o_slc = _diag_blocks(acc / l, rpg)
        _, l, acc = _flash_update(moba, qm, kmn_ref[0].astype(BF16), vmn_ref[0].astype(BF16), bias)
        om_ref[0] = _diag_blocks(acc / l, rpg)

        n_ctx = kw_ref.shape[2]
        wrel = lax.broadcasted_iota(jnp.int32, (1, n_ctx), 1) - (n_ctx - n_new)
        dist = tpos - wrel
        valid = (dist >= 0) & (dist < WINDOW) & (wrel + past >= 0)
        s = jnp.where(valid, _dot(qn, kw_ref[0].astype(BF16)) + slope * wrel.astype(F32), NEG_BIG)
        pw = jnp.exp(s - jnp.max(s, axis=-1, keepdims=True))
        o_win = _diag_blocks(_dot_t(pw.astype(BF16), vw_ref[0].astype(BF16)) / jnp.sum(pw, axis=-1, keepdims=True), rpg)

        gates = jax.nn.sigmoid(gt_ref[0])
        on_ref[0] = gates[:, 0:1] * ocmp_ref[0] + gates[:, 1:2] * o_slc + gates[:, 2:3] * o_win


def attn_sample(page_table, layer, qn, qm, sbias, mbias, slope, tpos, pools, new_kv, kwin_t, vwin_t, ocmp, gt,
                past, n_new, pages_per_step):
    b, rows, _ = qn.shape
    n_pages = page_table.shape[1]
    n_pool = pools[0].shape[0] // DEPTH
    pps = pages_per_step
    assert n_pages % pps == 0 and n_pages * PAGE_SIZE == past

    def spec(a):
        return pl.BlockSpec((1,) + a.shape[1:], lambda bi, s, pt: (bi,) + (0,) * (a.ndim - 1))

    def full(a):
        return pl.BlockSpec(a.shape, lambda bi, s, pt: (0,) * a.ndim)

    page_specs = [_page_spec(layer, n_pool, n_pages, pps, i) for _ in range(4) for i in range(pps)]
    page_args = [pool for pool in pools for _ in range(pps)]
    o_shape = jax.ShapeDtypeStruct((b, rows, HEAD_DIM), F32)
    return pl.pallas_call(
        functools.partial(_attn_sample_kernel, pages_per_step=pps, past=past, n_new=n_new),
        grid_spec=pltpu.PrefetchScalarGridSpec(
            num_scalar_prefetch=1,
            grid=(b, n_pages // pps),
            in_specs=[spec(qn), spec(qm), spec(sbias), spec(mbias), full(slope), full(tpos)] + page_specs
                     + [spec(a) for a in new_kv] + [spec(kwin_t), spec(vwin_t), spec(ocmp), spec(gt)],
            out_specs=[spec(o_shape), spec(o_shape)],
            scratch_shapes=[pltpu.VMEM((rows, 1), F32), pltpu.VMEM((rows, 1), F32), pltpu.VMEM((rows, KV_W), F32),
                            pltpu.VMEM((rows, 1), F32), pltpu.VMEM((rows, 1), F32), pltpu.VMEM((rows, KV_W), F32)],
        ),
        out_shape=[o_shape, o_shape],
        compiler_params=_cparams("parallel", "arbitrary"),
        name="attn_sample",
    )(page_table.reshape(-1), qn, qm, sbias, mbias, slope, tpos, *page_args, *new_kv, kwin_t, vwin_t, ocmp, gt)


def _out_proj(o_nsa, o_moba, gn_ref, gm_ref, wn_ref, wm_ref, x_ref, g_ref, b_ref, o_ref):
    hn = _rms_norm(o_nsa, gn_ref[...]).astype(BF16)
    hm = _rms_norm(o_moba, gm_ref[...]).astype(BF16)
    y = _dot(hn, wn_ref[...]) + _dot(hm, wm_ref[...])
    o_ref[...] = _layer_norm(DEEPNORM_ALPHA * x_ref[...] + y, g_ref[...], b_ref[...])


def _out_proj_kernel(on_ref, om_ref, *refs):
    _out_proj(on_ref[...], om_ref[...], *refs)


def _out_proj_fm_kernel(on_ref, om_ref, *refs):
    _out_proj(on_ref[0].T, om_ref[0].T, *refs)


def out_proj_ln(o_nsa, o_moba, gn_nsa, gn_moba, w_nsa, w_moba, x, g, b, feature_major):
    m = x.shape[0]
    assert m % ROW_TILE == 0
    row = lambda w: pl.BlockSpec((ROW_TILE, w), lambda i: (i, 0))
    full = lambda a: pl.BlockSpec(a.shape, lambda i: (0,) * a.ndim)
    if feature_major:
        per_batch = o_nsa.shape[2] // ROW_TILE
        o_spec = pl.BlockSpec((1, Q_W, ROW_TILE), lambda i: (i // per_batch, 0, i % per_batch))
    else:
        o_spec = row(Q_W)
    return pl.pallas_call(
        _out_proj_fm_kernel if feature_major else _out_proj_kernel,
        grid=(m // ROW_TILE,),
        in_specs=[o_spec, o_spec, full(gn_nsa), full(gn_moba), full(w_nsa), full(w_moba),
                  row(D_MODEL), full(g), full(b)],
        out_specs=row(D_MODEL),
        out_shape=jax.ShapeDtypeStruct((m, D_MODEL), F32),
        compiler_params=_cparams("parallel"),
        name="out_proj_ln",
    )(o_nsa, o_moba, gn_nsa, gn_moba, w_nsa, w_moba, x, g, b)


def _mem_attn_kernel(x_ref, wq_ref, k_ref, v_ref, wo_ref, g_ref, b_ref, o_ref):
    x = x_ref[0]
    q = _dot(x.astype(BF16), wq_ref[...]).astype(BF16)
    k = k_ref[0].astype(BF16)
    v = v_ref[0].astype(BF16)
    heads = []
    for h in range(MEM_HEADS):
        sl = slice(h * MEM_HEAD_DIM, (h + 1) * MEM_HEAD_DIM)
        s = _dot_t(q[:, sl], k[:, sl]) * (MEM_HEAD_DIM ** -0.5)
        e = jnp.exp(s - jnp.max(s, axis=-1, keepdims=True))
        p = e / jnp.sum(e, axis=-1, keepdims=True)
        heads.append(_dot(p.astype(BF16), v[:, sl]))
    o = jnp.concatenate(heads, axis=-1).astype(BF16)
    o_ref[0] = _layer_norm(DEEPNORM_ALPHA * x + _dot(o, wo_ref[...]), g_ref[...], b_ref[...])


def mem_attn_ln(x, wq, mem_k, mem_v, wo, g, b, tq):
    bsz, t, d = x.shape
    n_mem, d_mem = mem_k.shape[1:]
    assert t % tq == 0
    full = lambda a: pl.BlockSpec(a.shape, lambda bi, i: (0,) * a.ndim)
    return pl.pallas_call(
        _mem_attn_kernel,
        grid=(bsz, t // tq),
        in_specs=[pl.BlockSpec((1, tq, d), lambda bi, i: (bi, i, 0)), full(wq),
                  pl.BlockSpec((1, n_mem, d_mem), lambda bi, i: (bi, 0, 0)),
                  pl.BlockSpec((1, n_mem, d_mem), lambda bi, i: (bi, 0, 0)), full(wo), full(g), full(b)],
        out_specs=pl.BlockSpec((1, tq, d), lambda bi, i: (bi, i, 0)),
        out_shape=jax.ShapeDtypeStruct(x.shape, F32),
        compiler_params=_cparams("parallel", "arbitrary"),
        name="mem_attn_ln",
    )(x, wq, mem_k, mem_v, wo, g, b)


def _router_kernel(x_ref, w_ref, b_ref, e_ref, g_ref):
    logits = jnp.dot(x_ref[...], w_ref[...], preferred_element_type=F32, precision=lax.Precision.HIGHEST) + b_ref[...]
    lane = lax.broadcasted_iota(jnp.int32, logits.shape, 1).astype(F32)
    cur = logits
    vals, idxs = [], []
    for _ in range(TOP_K):
        m = jnp.max(cur, axis=-1, keepdims=True)
        idx = jnp.min(jnp.where(cur == m, lane, float(N_EXPERTS)), axis=-1, keepdims=True)
        vals.append(m)
        idxs.append(idx)
        cur = jnp.where(lane == idx, -jnp.inf, cur)
    top_v = jnp.concatenate(vals, axis=-1)
    e = jnp.exp(top_v - vals[0])
    g_ref[...] = e / jnp.sum(e, axis=-1, keepdims=True)
    e_ref[...] = jnp.concatenate(idxs, axis=-1).astype(jnp.int32)


def router(x, w, b):
    m, d = x.shape
    assert m % ROW_TILE == 0
    return pl.pallas_call(
        _router_kernel,
        grid=(m // ROW_TILE,),
        in_specs=[pl.BlockSpec((ROW_TILE, d), lambda i: (i, 0)),
                  pl.BlockSpec(w.shape, lambda i: (0, 0)), pl.BlockSpec(b.shape, lambda i: (0, 0))],
        out_specs=[pl.BlockSpec((ROW_TILE, TOP_K), lambda i: (i, 0)), pl.BlockSpec((ROW_TILE, TOP_K), lambda i: (i, 0))],
        out_shape=[jax.ShapeDtypeStruct((m, TOP_K), jnp.int32), jax.ShapeDtypeStruct((m, TOP_K), F32)],
        compiler_params=_cparams("parallel"),
        name="router",
    )(x, w, b)


def _expert_up_kernel(be_ref, nu_ref, x_ref, wg_ref, wu_ref, bg_ref, bu_ref, h_ref):
    del be_ref

    @pl.when(pl.program_id(1) < nu_ref[0])
    def _():
        x = x_ref[...]
        g = _dot(x, wg_ref[0].astype(BF16)) + bg_ref[0]
        u = _dot(x, wu_ref[0].astype(BF16)) + bu_ref[0]
        g = jnp.minimum(g, SWIGLU_LIMIT)
        u = jnp.clip(u, -SWIGLU_LIMIT, SWIGLU_LIMIT)
        h_ref[...] = (g * jax.nn.sigmoid(SWIGLU_ALPHA * g) * (u + 1.0)).astype(BF16)

    @pl.when(pl.program_id(1) >= nu_ref[0])
    def _():
        h_ref[...] = jnp.zeros(h_ref.shape, BF16)


def _expert_down_kernel(be_ref, nu_ref, h_ref, wd_ref, bd_ref, y_ref):
    del be_ref

    @pl.when(pl.program_id(1) < nu_ref[0])
    def _():
        y_ref[...] = _dot(h_ref[...], wd_ref[0].astype(BF16)) + bd_ref[0]

    @pl.when(pl.program_id(1) >= nu_ref[0])
    def _():
        y_ref[...] = jnp.zeros(y_ref.shape, F32)


def expert_ffn(xs, blk_e, n_used, layer, wg, bg, wu, bu, wd, bd):
    cap, d = xs.shape
    n_blocks = cap // MOE_TILE
    d_ff = wg.shape[-1]
    we = lambda j, i, be, nu: layer * N_EXPERTS + be[i]
    h = pl.pallas_call(
        _expert_up_kernel,
        grid_spec=pltpu.PrefetchScalarGridSpec(
            num_scalar_prefetch=2,
            grid=(d_ff // FF_CHUNK, n_blocks),
            in_specs=[pl.BlockSpec((MOE_TILE, d), lambda j, i, be, nu: (i, 0)),
                      pl.BlockSpec((1, d, FF_CHUNK), lambda j, i, be, nu: (we(j, i, be, nu), 0, j)),
                      pl.BlockSpec((1, d, FF_CHUNK), lambda j, i, be, nu: (we(j, i, be, nu), 0, j)),
                      pl.BlockSpec((1, 1, FF_CHUNK), lambda j, i, be, nu: (we(j, i, be, nu), 0, j)),
                      pl.BlockSpec((1, 1, FF_CHUNK), lambda j, i, be, nu: (we(j, i, be, nu), 0, j))],
            out_specs=pl.BlockSpec((MOE_TILE, FF_CHUNK), lambda j, i, be, nu: (i, j)),
        ),
        out_shape=jax.ShapeDtypeStruct((cap, d_ff), BF16),
        compiler_params=_cparams("parallel", "arbitrary"),
        name="expert_up",
    )(blk_e, n_used, xs, wg, wu, bg, bu)
    return pl.pallas_call(
        _expert_down_kernel,
        grid_spec=pltpu.PrefetchScalarGridSpec(
            num_scalar_prefetch=2,
            grid=(d // DOWN_CHUNK, n_blocks),
            in_specs=[pl.BlockSpec((MOE_TILE, d_ff), lambda j, i, be, nu: (i, 0)),
                      pl.BlockSpec((1, d_ff, DOWN_CHUNK), lambda j, i, be, nu: (we(j, i, be, nu), 0, j)),
                      pl.BlockSpec((1, 1, DOWN_CHUNK), lambda j, i, be, nu: (we(j, i, be, nu), 0, j))],
            out_specs=pl.BlockSpec((MOE_TILE, DOWN_CHUNK), lambda j, i, be, nu: (i, j)),
        ),
        out_shape=jax.ShapeDtypeStruct((cap, d), F32),
        compiler_params=_cparams("parallel", "arbitrary"),
        name="expert_down",
    )(blk_e, n_used, h, wd, bd)


def _residual_ln_kernel(x_ref, m_ref, g_ref, b_ref, o_ref):
    o_ref[...] = _layer_norm(DEEPNORM_ALPHA * x_ref[...] + m_ref[...], g_ref[...], b_ref[...])


def residual_ln(x, m, g, b):
    n, d = x.shape
    row = pl.BlockSpec((ROW_TILE, d), lambda i: (i, 0))
    full = lambda a: pl.BlockSpec(a.shape, lambda i: (0,) * a.ndim)
    return pl.pallas_call(
        _residual_ln_kernel,
        grid=(n // ROW_TILE,),
        in_specs=[row, row, full(g), full(b)],
        out_specs=row,
        out_shape=jax.ShapeDtypeStruct((n, d), F32),
        compiler_params=_cparams("parallel"),
        name="residual_ln",
    )(x, m, g, b)


def moe_ffn(x, layer, router_w, router_b, wg, bg, wu, bu, wd, bd):
    n, d = x.shape
    top_e, gate = router(x, router_w, router_b)
    a = n * TOP_K
    e_flat = top_e.reshape(a)
    onehot = (e_flat[:, None] == jnp.arange(N_EXPERTS, dtype=jnp.int32)[None, :]).astype(jnp.int32)
    csum = jnp.cumsum(onehot, axis=0)
    rank = jnp.sum((csum - onehot) * onehot, axis=1)
    counts = csum[-1]
    padded = (counts + MOE_TILE - 1) // MOE_TILE * MOE_TILE
    pad_end = jnp.cumsum(padded)
    pad_start = pad_end - padded
    dest = pad_start[e_flat] + rank
    n_blocks = -(-a // MOE_TILE) + N_EXPERTS
    cap = n_blocks * MOE_TILE
    tok = jnp.arange(a, dtype=jnp.int32) // TOP_K
    src = jnp.zeros((cap,), jnp.int32).at[dest].set(tok)
    blk_start = jnp.arange(n_blocks, dtype=jnp.int32) * MOE_TILE
    blk_e = jnp.minimum(jnp.sum(blk_start[:, None] >= pad_end[None, :], axis=1), N_EXPERTS - 1).astype(jnp.int32)
    n_used = (pad_end[-1] // MOE_TILE).astype(jnp.int32).reshape(1)
    xs = x.astype(BF16)[src]
    y = expert_ffn(xs, blk_e, n_used, layer, wg, bg, wu, bu, wd, bd)
    return jnp.sum(y[dest.reshape(n, TOP_K).T] * gate.T[:, :, None], axis=0)


def _alibi_slopes():
    return jnp.asarray(2.0 ** (-8.0 * np.arange(1, N_HEADS + 1) / N_HEADS), F32)


def _split_w_in(w):
    q_n, kv_n, gt, q_m, kv_m = jnp.split(w, [Q_W, Q_W + 6 * KV_W, Q_W + 6 * KV_W + 3 * N_HEADS,
                                            2 * Q_W + 6 * KV_W + 3 * N_HEADS], axis=1)
    kc, vc, ks, vs, kw, vw = jnp.split(kv_n, 6, axis=1)
    km, vm = jnp.split(kv_m, 2, axis=1)
    return q_n, (kc, vc, ks, vs, kw, vw, km, vm), gt, q_m


def _block_diag_rows(q, b, t):
    q = (q * (HEAD_DIM ** -0.5)).astype(BF16).reshape(b, t, N_KV, N_REP, HEAD_DIM).transpose(0, 2, 3, 1, 4)
    eye = jnp.eye(N_KV, dtype=BF16)
    bd = q[:, :, :, :, None, :] * eye[None, :, None, None, :, None]
    return bd.reshape(b, N_HEADS * t, KV_W)


def _rows_to_tokens(o, b, t):
    return o.reshape(b, N_KV, N_REP, t, HEAD_DIM).transpose(0, 3, 1, 2, 4).reshape(b * t, Q_W)


def _token_minor(a):
    lead = a.shape[:-2]
    a = a.reshape(lead + (N_KV, HEAD_DIM, a.shape[-1]))
    n = len(lead)
    return a.transpose(tuple(range(n)) + (n + 2, n, n + 1))


def kernel(x_prompt, x_sample, cache_nsa_cmp_k, cache_nsa_cmp_v, cache_nsa_slc_k, cache_nsa_slc_v, cache_moba_k, cache_moba_v, cache_nsa_win_k, cache_nsa_win_v, cache_mem_k, cache_mem_v, page_table, mem_prompt, w_in, nsa_cmp_pos, nsa_cmp_k_w1, nsa_cmp_k_w2, nsa_cmp_v_w1, nsa_cmp_v_w2, gn_nsa, gn_moba, w_out, ln1_g, ln1_b, ca_wq, ca_wk, ca_wv, ca_wo, ln2_g, ln2_b, router_w, router_b, exp_wg, exp_bg, exp_wu, exp_bu, exp_wd, exp_bd, ln3_g, ln3_b):
    bp, tp, d = x_prompt.shape
    bs, ts, _ = x_sample.shape
    n_p, n_s = bp * tp, bs * ts
    n_mem = mem_prompt.shape[1]
    depth, n_pool = cache_nsa_cmp_k.shape[:2]
    n_pages = page_table.shape[1]
    past = n_pages * PAGE_SIZE
    wbuf = cache_nsa_win_k.shape[2]
    d_mem = MEM_HEADS * MEM_HEAD_DIM
    slopes = _alibi_slopes()
    fm = lambda c: c.transpose(0, 1, 3, 4, 2).reshape(c.shape[0], c.shape[1], KV_W, c.shape[2])
    pool = lambda c: fm(c).reshape(depth * n_pool, KV_W, PAGE_SIZE)
    pools = tuple(map(pool, (cache_nsa_slc_k, cache_nsa_slc_v, cache_moba_k, cache_moba_v)))
    cmp_pools = tuple(map(pool, (cache_nsa_cmp_k, cache_nsa_cmp_v)))
    win_k_fm, win_v_fm = fm(cache_nsa_win_k), fm(cache_nsa_win_v)
    wg = exp_wg.reshape((depth * N_EXPERTS,) + exp_wg.shape[2:])
    wu = exp_wu.reshape((depth * N_EXPERTS,) + exp_wu.shape[2:])
    wd = exp_wd.reshape((depth * N_EXPERTS,) + exp_wd.shape[2:])
    bg = exp_bg.reshape(depth * N_EXPERTS, 1, -1)
    bu = exp_bu.reshape(depth * N_EXPERTS, 1, -1)
    bd = exp_bd.reshape(depth * N_EXPERTS, 1, -1)

    row = np.arange(N_HEADS * ts)
    s_slope = slopes[row // ts].reshape(-1, 1)
    s_tpos = jnp.asarray((row % ts).reshape(-1, 1), jnp.int32)
    n_cmp_s = (past + ts) // CMP_BLOCK
    n_sel_lanes = -(-(-(-(past + ts) // SEL_BLOCK)) // 128) * 128
    n_moba_past = past // MOBA_BLOCK

    x_p = x_prompt
    x_s = x_sample.reshape(n_s, d)
    p_states, s_states, mem_states = [], [], []
    s_win = []
    for l in range(depth):
        q_n, kv, gt, q_m = _split_w_in(w_in[l])
        kc, vc, ks, vs, kw, vw, km, vm = kv
        pos = nsa_cmp_pos[l].reshape(1, CMP_BLOCK * HEAD_DIM)
        ck_w1, ck_w2 = nsa_cmp_k_w1[l].astype(BF16), nsa_cmp_k_w2[l].astype(BF16)
        cv_w1, cv_w2 = nsa_cmp_v_w1[l].astype(BF16), nsa_cmp_v_w2[l].astype(BF16)

        wq_t = jnp.concatenate([q_n, q_m], axis=1).T.astype(BF16)
        wkv_t = jnp.concatenate([kc, vc, ks, vs, km, vm, kw, vw], axis=1).T.astype(BF16)
        wgt_t = jnp.pad(gt.reshape(d, N_KV, 3 * N_REP), ((0, 0), (0, 0), (0, GATE_PAD - 3 * N_REP)))
        wgt_t = wgt_t.reshape(d, N_KV * GATE_PAD).T.astype(BF16)
        wk_tm = jnp.concatenate([kc, vc, ks, kw, km], axis=1).astype(BF16)
        (q_fm,) = proj_feature_major(x_p, wq_t, (BF16,), HEAD_DIM ** -0.5)
        st_fm, kv_fm = proj_feature_major(x_p, wkv_t, (F32, BF16))
        (gt_fm,) = proj_feature_major(x_p, wgt_t, (F32,))
        kc_tm, vc_tm, k_tm = proj_keys_token_major(x_p, wk_tm)
        n_cmp_p = tp // CMP_BLOCK
        k_cmp = cmp_mlp(kc_tm.reshape(bp * N_KV * n_cmp_p, CMP_BLOCK * HEAD_DIM), pos, ck_w1, ck_w2, 256)
        v_cmp = cmp_mlp(vc_tm.reshape(bp * N_KV * n_cmp_p, CMP_BLOCK * HEAD_DIM), pos, cv_w1, cv_w2, 256)
        k_cmp = k_cmp.reshape(bp, N_KV, n_cmp_p, HEAD_DIM)
        v_cmp_t = v_cmp.reshape(bp, N_KV, n_cmp_p, HEAD_DIM).transpose(0, 1, 3, 2)
        o_nsa_p, o_moba_p = attn_prompt(slopes, q_fm, k_tm, kv_fm, k_cmp, v_cmp_t, gt_fm)

        w_main = jnp.concatenate([q_n, kc, vc, ks, vs, kw, vw, q_m, km, vm], axis=1).astype(BF16)
        w_gate = jnp.pad(gt, ((0, 0), (0, 128 - 3 * N_HEADS))).astype(BF16)
        proj_s = matmul(x_s, w_main, 256, 512)
        gt_s = matmul(x_s, w_gate, 256, 128)[:, :3 * N_HEADS]
        qs_n, kc_s, vc_s, ks_s, vs_s, kw_s, vw_s, qs_m, km_s, vm_s = jnp.split(
            proj_s, [Q_W + KV_W * i for i in range(7)] + [2 * Q_W + 6 * KV_W, 2 * Q_W + 7 * KV_W], axis=1)
        sn = lambda a: a.reshape(bs, ts, KV_W)

        pos_t = jnp.tile(nsa_cmp_pos[l].T, (1, PAGE_SIZE // CMP_BLOCK))
        blocks = lambda a: a.transpose(0, 2, 1, 3).reshape(bs, n_cmp_s, KV_W)
        k_cmp_s = blocks(cmp_pages(cmp_pools[0], l, page_table, pos_t, _cmp_w1_by_dim(nsa_cmp_k_w1[l]), ck_w2))
        v_cmp_s = blocks(cmp_pages(cmp_pools[1], l, page_table, pos_t, _cmp_w1_by_dim(nsa_cmp_v_w1[l]), cv_w2))
        k_mean_t = moba_block_sums(pools[2], l, page_table, 8, 128)
        k_mean_t = k_mean_t.at[:, :, n_moba_past].set(sn(km_s).sum(axis=1)) * (1.0 / MOBA_BLOCK)
        qn_s = _block_diag_rows(qs_n, bs, ts)
        qm_s = _block_diag_rows(qs_m, bs, ts)
        o_cmp_s, sbias, mbias = sample_select(qn_s, qm_s, k_cmp_s, v_cmp_s, k_mean_t, s_slope, s_tpos, past, ts, n_sel_lanes)
        kwin_t = jnp.concatenate([win_k_fm[l], sn(kw_s).transpose(0, 2, 1)], axis=2)
        vwin_t = jnp.concatenate([win_v_fm[l], sn(vw_s).transpose(0, 2, 1)], axis=2)
        gt_rows = gt_s.reshape(bs, ts, N_KV, N_REP, 3).transpose(0, 2, 3, 1, 4).reshape(bs, N_HEADS * ts, 3)
        o_nsa_s, o_moba_s = attn_sample(
            page_table, l, qn_s, qm_s, sbias, mbias, s_slope, s_tpos, pools, (sn(ks_s), sn(vs_s), sn(km_s), sn(vm_s)),
            kwin_t, vwin_t, o_cmp_s, gt_rows, past, ts, 8)

        w_o = w_out[l].astype(BF16)
        op_args = (gn_nsa[l].reshape(1, -1), gn_moba[l].reshape(1, -1), w_o[:Q_W], w_o[Q_W:])
        ln1 = (ln1_g[l].reshape(1, -1), ln1_b[l].reshape(1, -1))
        x_p = out_proj_ln(o_nsa_p, o_moba_p, *op_args, x_p.reshape(n_p, d), *ln1, True)
        x_s = out_proj_ln(_rows_to_tokens(o_nsa_s, bs, ts), _rows_to_tokens(o_moba_s, bs, ts), *op_args, x_s, *ln1, False)

        mem_kv = matmul(mem_prompt.reshape(bp * n_mem, d), jnp.concatenate([ca_wk[l], ca_wv[l]], axis=1).astype(BF16), 512, 512)
        mem_k, mem_v = mem_kv[:, :d_mem].reshape(bp, n_mem, d_mem), mem_kv[:, d_mem:].reshape(bp, n_mem, d_mem)
        wq, wo = ca_wq[l].astype(BF16), ca_wo[l].astype(BF16)
        g2, b2 = ln2_g[l].reshape(1, -1), ln2_b[l].reshape(1, -1)
        x_p = mem_attn_ln(x_p.reshape(bp, tp, d), wq, mem_k, mem_v, wo, g2, b2, ROW_TILE)
        x_s = mem_attn_ln(x_s.reshape(bs, ts, d), wq, cache_mem_k[l].reshape(bs, -1, d_mem),
                          cache_mem_v[l].reshape(bs, -1, d_mem), wo, g2, b2, ts)

        x = jnp.concatenate([x_p.reshape(n_p, d), x_s.reshape(n_s, d)], axis=0)
        m = moe_ffn(x, l, router_w[l], router_b[l].reshape(1, -1), wg, bg, wu, bu, wd, bd)
        x = residual_ln(x, m, ln3_g[l].reshape(1, -1), ln3_b[l].reshape(1, -1))
        x_p, x_s = x[:n_p].reshape(bp, tp, d), x[n_p:]

        p_states.append(st_fm)
        mem_states.append((mem_k.reshape(bp, n_mem, MEM_HEADS, MEM_HEAD_DIM), mem_v.reshape(bp, n_mem, MEM_HEADS, MEM_HEAD_DIM)))
        s_states.append([a.reshape(bs, ts, N_KV, HEAD_DIM) for a in (kc_s, vc_s, ks_s, vs_s, km_s, vm_s)])
        keep_s = min(WINDOW, wbuf + ts)
        s_win.append((kwin_t[:, :, -keep_s:], vwin_t[:, :, -keep_s:]))

    keep = min(WINDOW, tp)
    st = jnp.stack(p_states)
    p_out = [_token_minor(st[:, :, i * KV_W:(i + 1) * KV_W, :]) for i in range(6)]
    p_out += [_token_minor(st[:, :, i * KV_W:(i + 1) * KV_W, tp - keep:]) for i in (6, 7)]
    p_out += [jnp.stack([ms[i] for ms in mem_states]) for i in range(2)]
    s_out = [jnp.stack([ss[i] for ss in s_states]) for i in range(6)]
    s_out += [_token_minor(jnp.stack([w[i] for w in s_win])) for i in range(2)]
    return (x_p, x_s.reshape(bs, ts, d)) + tuple(p_out) + tuple(s_out)
```

```python
import functools

import numpy as np
import jax
import jax.numpy as jnp
from jax import lax
from jax.experimental import pallas as pl
from jax.experimental.pallas import tpu as pltpu

F32 = jnp.float32
BF16 = jnp.bfloat16

D_MODEL = 2048
HEAD_DIM = 64
N_KV = 4
N_REP = 4
N_HEADS = N_KV * N_REP
CMP_BLOCK = 64
SEL_BLOCK = 64
N_SEL = 8
WINDOW = 512
MOBA_BLOCK = 256
MOBA_TOPK = 3
MEM_HEADS = 4
MEM_HEAD_DIM = 128
N_EXPERTS = 32
TOP_K = 4
SWIGLU_LIMIT = 7.0
SWIGLU_ALPHA = 1.702
LN_EPS = 1e-5
NEG_BIG = -1e30
DEPTH = 2
DEEPNORM_ALPHA = (2 * DEPTH) ** 0.25
PAGE_SIZE = 128

KV_W = N_KV * HEAD_DIM
Q_W = N_HEADS * HEAD_DIM
N_STATE = 8
GATE_PAD = 16
ALIBI_PARTS = 3
BLOCK_ROW0 = 16
TQ = 256
VMEM_LIMIT = 56 * 1024 * 1024
ROW_TILE = 256
PROJ_TILE = 512
MOE_TILE = 512
FF_CHUNK = 512
DOWN_CHUNK = 1024


def _cparams(*sem):
    return pltpu.CompilerParams(dimension_semantics=sem, vmem_limit_bytes=VMEM_LIMIT)


def _dot(a, b):
    return jnp.dot(a, b, preferred_element_type=F32)


def _dot_t(a, b):
    return lax.dot_general(a, b, (((1,), (1,)), ((), ())), preferred_element_type=F32)


def _div_pow2(x, n):
    assert n & (n - 1) == 0
    return lax.shift_right_logical(x, n.bit_length() - 1)


def _layer_norm(z, g, b):
    mu = jnp.mean(z, axis=-1, keepdims=True)
    zc = z - mu
    var = jnp.mean(zc * zc, axis=-1, keepdims=True)
    return zc * lax.rsqrt(var + LN_EPS) * g + b


def _rms_norm(x, g):
    return x * lax.rsqrt(jnp.mean(x * x, axis=-1, keepdims=True) + LN_EPS) * g


def _masked_softmax(s, mask, axis):
    s = jnp.where(mask, s, NEG_BIG)
    m = jnp.max(s, axis=axis, keepdims=True)
    e = jnp.where(mask, jnp.exp(s - m), 0.0)
    den = jnp.sum(e, axis=axis, keepdims=True)
    return e / jnp.where(den > 0.0, den, 1.0)


def _topk_bias(score, k, index, axis):
    index = index.astype(F32)
    cur = score
    bias = jnp.full(score.shape, NEG_BIG, F32)
    for _ in range(k):
        m = jnp.max(cur, axis=axis, keepdims=True)
        first = jnp.min(jnp.where(cur == m, index, float(score.shape[axis])), axis=axis, keepdims=True)
        pick = index == first
        bias = jnp.where(pick & (m > -jnp.inf), 0.0, bias)
        cur = jnp.where(pick, -jnp.inf, cur)
    return bias


def _mm_kernel(x_ref, w_ref, o_ref):
    o_ref[...] = _dot(x_ref[...].astype(BF16), w_ref[...])


def matmul(x, w, tm, tn):
    m, k = x.shape
    n = w.shape[1]
    tm = max(t for t in range(8, min(tm, m) + 1, 8) if m % t == 0)
    assert n % tn == 0
    return pl.pallas_call(
        _mm_kernel,
        grid=(m // tm, n // tn),
        in_specs=[pl.BlockSpec((tm, k), lambda i, j: (i, 0)),
                  pl.BlockSpec((k, tn), lambda i, j: (0, j))],
        out_specs=pl.BlockSpec((tm, tn), lambda i, j: (i, j)),
        out_shape=jax.ShapeDtypeStruct((m, n), F32),
        compiler_params=_cparams("parallel", "arbitrary"),
        name="matmul",
    )(x, w)


def _proj_fm_kernel(x_ref, w_ref, *o_refs, scale):
    r = _dot_t(w_ref[...], x_ref[0].astype(BF16))
    if scale != 1.0:
        r = r * scale
    for o_ref in o_refs:
        o_ref[0] = r.astype(o_ref.dtype)


def proj_feature_major(x, w_t, dtypes, scale=1.0):
    b, t, k = x.shape
    f = w_t.shape[0]
    tf = min(PROJ_TILE, f)
    assert t % PROJ_TILE == 0 and f % tf == 0
    o_spec = pl.BlockSpec((1, tf, PROJ_TILE), lambda bi, j, fi: (bi, fi, j))
    return pl.pallas_call(
        functools.partial(_proj_fm_kernel, scale=scale),
        grid=(b, t // PROJ_TILE, f // tf),
        in_specs=[pl.BlockSpec((1, PROJ_TILE, k), lambda bi, j, fi: (bi, j, 0)),
                  pl.BlockSpec((tf, k), lambda bi, j, fi: (fi, 0))],
        out_specs=[o_spec for _ in dtypes],
        out_shape=[jax.ShapeDtypeStruct((b, f, t), dt) for dt in dtypes],
        compiler_params=_cparams("parallel", "parallel", "arbitrary"),
        name="proj_feature_major",
    )(x, w_t)


def _proj_keys_kernel(x_ref, w_ref, kc_ref, vc_ref, k_ref):
    r = _dot(x_ref[0].astype(BF16), w_ref[...])
    for g in range(N_KV):
        lo = g * HEAD_DIM
        kc_ref[0, g] = r[:, lo:lo + HEAD_DIM]
        vc_ref[0, g] = r[:, KV_W + lo:KV_W + lo + HEAD_DIM]
        for w in range(3):
            k_ref[0, w, g] = r[:, (2 + w) * KV_W + lo:(2 + w) * KV_W + lo + HEAD_DIM].astype(BF16)


def proj_keys_token_major(x, w):
    b, t, k = x.shape
    assert t % PROJ_TILE == 0 and w.shape[1] == 5 * KV_W
    c_spec = pl.BlockSpec((1, N_KV, PROJ_TILE, HEAD_DIM), lambda bi, j: (bi, 0, j, 0))
    c_shape = jax.ShapeDtypeStruct((b, N_KV, t, HEAD_DIM), F32)
    return pl.pallas_call(
        _proj_keys_kernel,
        grid=(b, t // PROJ_TILE),
        in_specs=[pl.BlockSpec((1, PROJ_TILE, k), lambda bi, j: (bi, j, 0)),
                  pl.BlockSpec(w.shape, lambda bi, j: (0, 0))],
        out_specs=[c_spec, c_spec, pl.BlockSpec((1, 3, N_KV, PROJ_TILE, HEAD_DIM), lambda bi, j: (bi, 0, 0, j, 0))],
        out_shape=[c_shape, c_shape, jax.ShapeDtypeStruct((b, 3, N_KV, t, HEAD_DIM), BF16)],
        compiler_params=_cparams("parallel", "arbitrary"),
        name="proj_keys_token_major",
    )(x, w)


def _cmp_mlp_kernel(x_ref, pos_ref, w1_ref, w2_ref, o_ref):
    x = (x_ref[...] + pos_ref[...]).astype(BF16)
    h = _dot(x, w1_ref[...])
    h = h * jax.nn.sigmoid(h)
    o_ref[...] = _dot(h.astype(BF16), w2_ref[...])


def cmp_mlp(x, pos, w1, w2, tr):
    r, k = x.shape
    hid = w1.shape[1]
    tr = min(tr, r)
    assert r % tr == 0
    return pl.pallas_call(
        _cmp_mlp_kernel,
        grid=(r // tr,),
        in_specs=[pl.BlockSpec((tr, k), lambda i: (i, 0)),
                  pl.BlockSpec((1, k), lambda i: (0, 0)),
                  pl.BlockSpec((k, hid), lambda i: (0, 0)),
                  pl.BlockSpec((hid, HEAD_DIM), lambda i: (0, 0))],
        out_specs=pl.BlockSpec((tr, HEAD_DIM), lambda i: (i, 0)),
        out_shape=jax.ShapeDtypeStruct((r, HEAD_DIM), F32),
        compiler_params=_cparams("parallel"),
        name="cmp_mlp",
    )(x, pos, w1, w2)


def _flash_t(carry, k_tile, q_aug, vt_tile, mask):
    m, l, acc = carry
    s = _dot(k_tile, q_aug)
    if mask is not None:
        s = jnp.where(mask, s, NEG_BIG)
    m_new = jnp.maximum(m, jnp.max(s, axis=0, keepdims=True))
    alpha = jnp.exp(m - m_new)
    p = jnp.exp(s - m_new)
    l = alpha * l + jnp.sum(p, axis=0, keepdims=True)
    acc = alpha * acc + _dot(vt_tile, p.astype(BF16))
    return m_new, l, acc


def _attn_prompt_kernel(slopes_ref, qn_ref, qm_ref, ks_ref, kw_ref, km_ref, vs_ref, vw_ref, vm_ref,
                        kc_ref, vct_ref, gt_ref, on_ref, om_ref, kaug_ref, kmean_ref, *, seq):
    g = pl.program_id(1)
    qt = pl.program_id(2)
    n_cmp = seq // CMP_BLOCK
    n_moba = seq // MOBA_BLOCK
    t0 = qt * TQ

    @pl.when(qt == 0)
    def _():
        ipos = lax.broadcasted_iota(jnp.int32, (seq, HEAD_DIM), 0)
        pos = ipos.astype(F32)
        lane = lax.broadcasted_iota(jnp.int32, (seq, HEAD_DIM), 1)
        cols = jnp.zeros((seq, HEAD_DIM), F32)
        for r in range(N_REP):
            v = pos * slopes_ref[g * N_REP + r]
            hi = v.astype(BF16).astype(F32)
            mid = (v - hi).astype(BF16).astype(F32)
            lo = v - hi - mid
            c = ALIBI_PARTS * r
            cols = jnp.where(lane == c, hi, jnp.where(lane == c + 1, mid, jnp.where(lane == c + 2, lo, cols)))
        in_slc_block = lane == BLOCK_ROW0 + _div_pow2(ipos, SEL_BLOCK)
        in_moba_block = lane == BLOCK_ROW0 + _div_pow2(ipos, MOBA_BLOCK)
        for w, (k_ref, onehot) in enumerate(((ks_ref, in_slc_block), (kw_ref, None), (km_ref, in_moba_block))):
            extra = cols if onehot is None else jnp.where(onehot, 1.0, cols)
            kaug_ref[w] = jnp.concatenate([k_ref[0, 0, 0], extra.astype(BF16)], axis=1)
        km = km_ref[0, 0, 0].astype(F32)
        kmean_ref[...] = jnp.mean(km.reshape(n_moba, MOBA_BLOCK, HEAD_DIM), axis=1)

    wide = N_REP * TQ
    per_rep = lambda f: jnp.concatenate([f(r) for r in range(N_REP)], axis=1)
    q_in_tile = lax.broadcasted_iota(jnp.int32, (1, wide), 1) & (TQ - 1)
    tq = t0 + q_in_tile
    causal = lax.broadcasted_iota(jnp.int32, (TQ, wide), 0) <= q_in_tile
    gates = jax.nn.sigmoid(gt_ref[0])
    gate_row = lambda branch: per_rep(lambda r: gates[3 * r + branch:3 * r + branch + 1, :])
    slope_row = per_rep(lambda r: jnp.full((1, TQ), slopes_ref[g * N_REP + r], F32))
    aug_row = lax.broadcasted_iota(jnp.int32, (HEAD_DIM, wide), 0)
    aug_lo = ALIBI_PARTS * _div_pow2(lax.broadcasted_iota(jnp.int32, (HEAD_DIM, wide), 1), TQ)
    ones = jnp.where((aug_row >= aug_lo) & (aug_row < aug_lo + ALIBI_PARTS), 1.0, 0.0)[:BLOCK_ROW0]
    qn = per_rep(lambda r: qn_ref[0, r * HEAD_DIM:(r + 1) * HEAD_DIM, :])
    qm = per_rep(lambda r: qm_ref[0, r * HEAD_DIM:(r + 1) * HEAD_DIM, :])
    init = (jnp.full((1, wide), NEG_BIG, F32), jnp.zeros((1, wide), F32), jnp.zeros((HEAD_DIM, wide), F32))

    def augment(q, block_bias):
        pad = jnp.zeros((HEAD_DIM - BLOCK_ROW0 - block_bias.shape[0], wide), F32)
        return jnp.concatenate([q, jnp.concatenate([ones, block_bias, pad], axis=0).astype(BF16)], axis=0)

    kc = kc_ref[0, 0].astype(BF16)
    vct = vct_ref[0, 0].astype(BF16)
    k_end = lax.broadcasted_iota(jnp.int32, (n_cmp, wide), 0) * CMP_BLOCK + (CMP_BLOCK - 1)
    p = _masked_softmax(_dot(kc, qn) + slope_row * k_end.astype(F32), k_end <= tq, 0)
    o_cmp = _dot(vct, p.astype(BF16))
    imp = p[:, 0:TQ]
    for r in range(1, N_REP):
        imp = imp + p[:, r * TQ:(r + 1) * TQ]

    blk = lax.broadcasted_iota(jnp.int32, (n_cmp, TQ), 0)
    tq1 = tq[:, 0:TQ]
    forced = (blk == _div_pow2(tq1, SEL_BLOCK)) | (blk == 0)
    visible = (blk * CMP_BLOCK + (CMP_BLOCK - 1)) <= tq1
    score = jnp.where(forced, jnp.inf, jnp.where(visible, imp, -jnp.inf))
    qn_aug = augment(qn, jnp.concatenate([_topk_bias(score, N_SEL, blk, 0)] * N_REP, axis=1))

    mblk = lax.broadcasted_iota(jnp.int32, (n_moba, wide), 0)
    gate = _dot(kmean_ref[...].astype(BF16), qm)
    moba_bias = _topk_bias(jnp.where(mblk < qt, gate, -jnp.inf), MOBA_TOPK, mblk, 0)
    qm_aug = augment(qm, jnp.where(mblk == qt, 0.0, moba_bias))

    def tile_body(kt, carry):
        start = pl.multiple_of(kt * TQ, TQ)
        slc = _flash_t(carry[0], kaug_ref[0, pl.ds(start, TQ), :], qn_aug, vs_ref[0, :, pl.ds(start, TQ)], None)
        moba = _flash_t(carry[1], kaug_ref[2, pl.ds(start, TQ), :], qm_aug, vm_ref[0, :, pl.ds(start, TQ)], None)
        return slc, moba

    slc, moba = lax.fori_loop(0, qt, tile_body, (init, init))
    d_start = pl.multiple_of(t0, TQ)
    _, l, acc = _flash_t(slc, kaug_ref[0, pl.ds(d_start, TQ), :], qn_aug, vs_ref[0, :, pl.ds(d_start, TQ)], causal)
    o_slc = acc / l
    _, l, acc = _flash_t(moba, kaug_ref[2, pl.ds(d_start, TQ), :], qm_aug, vm_ref[0, :, pl.ds(d_start, TQ)], causal)
    o_moba = acc / l

    w_len = WINDOW + TQ
    w_start = pl.multiple_of(jnp.maximum(qt - 2, 0) * TQ, TQ)
    wpos = w_start + lax.broadcasted_iota(jnp.int32, (w_len, wide), 0)
    s = jnp.where((wpos <= tq) & (wpos > tq - WINDOW), _dot(kaug_ref[1, pl.ds(w_start, w_len), :], qn_aug), NEG_BIG)
    pw = jnp.exp(s - jnp.max(s, axis=0, keepdims=True))
    o_win = _dot(vw_ref[0, :, pl.ds(w_start, w_len)], pw.astype(BF16)) / jnp.sum(pw, axis=0, keepdims=True)

    o_nsa = gate_row(0) * o_cmp + gate_row(1) * o_slc + gate_row(2) * o_win
    for r in range(N_REP):
        on_ref[0, r * HEAD_DIM:(r + 1) * HEAD_DIM, :] = o_nsa[:, r * TQ:(r + 1) * TQ]
        om_ref[0, r * HEAD_DIM:(r + 1) * HEAD_DIM, :] = o_moba[:, r * TQ:(r + 1) * TQ]


def attn_prompt(slopes, q_fm, k_tm, kv_fm, k_cmp, v_cmp_t, gt_fm):
    b, _, seq = q_fm.shape
    assert MOBA_BLOCK == TQ and seq % TQ == 0 and seq >= WINDOW + TQ
    n_cmp = seq // CMP_BLOCK
    assert ALIBI_PARTS * N_REP <= BLOCK_ROW0 and BLOCK_ROW0 + n_cmp <= HEAD_DIM
    q_spec = lambda off: pl.BlockSpec((1, KV_W, TQ), lambda bi, g, qt: (bi, off + g, qt))
    k_spec = lambda w: pl.BlockSpec((1, 1, 1, seq, HEAD_DIM), lambda bi, g, qt: (bi, w, g, 0, 0))
    v_spec = lambda state: pl.BlockSpec((1, HEAD_DIM, seq), lambda bi, g, qt: (bi, state * N_KV + g, 0))
    o_spec = pl.BlockSpec((1, KV_W, TQ), lambda bi, g, qt: (bi, g, qt))
    o_shape = jax.ShapeDtypeStruct((b, Q_W, seq), F32)
    return pl.pallas_call(
        functools.partial(_attn_prompt_kernel, seq=seq),
        grid=(b, N_KV, seq // TQ),
        in_specs=[pl.BlockSpec(memory_space=pltpu.SMEM), q_spec(0), q_spec(N_KV),
                  k_spec(0), k_spec(1), k_spec(2), v_spec(3), v_spec(7), v_spec(5),
                  pl.BlockSpec((1, 1, n_cmp, HEAD_DIM), lambda bi, g, qt: (bi, g, 0, 0)),
                  pl.BlockSpec((1, 1, HEAD_DIM, n_cmp), lambda bi, g, qt: (bi, g, 0, 0)),
                  pl.BlockSpec((1, GATE_PAD, TQ), lambda bi, g, qt: (bi, g, qt))],
        out_specs=[o_spec, o_spec],
        out_shape=[o_shape, o_shape],
        scratch_shapes=[pltpu.VMEM((3, seq, 2 * HEAD_DIM), BF16), pltpu.VMEM((seq // MOBA_BLOCK, HEAD_DIM), F32)],
        compiler_params=_cparams("parallel", "parallel", "arbitrary"),
        name="attn_prompt",
    )(slopes, q_fm, q_fm, k_tm, k_tm, k_tm, kv_fm, kv_fm, kv_fm, k_cmp, v_cmp_t, gt_fm)


def _flash_update(carry, q, k, v, bias):
    m, l, acc = carry
    s = _dot_t(q, k) + bias
    m_new = jnp.maximum(m, jnp.max(s, axis=-1, keepdims=True))
    alpha = jnp.exp(m - m_new)
    p = jnp.exp(s - m_new)
    l = alpha * l + jnp.sum(p, axis=-1, keepdims=True)
    return m_new, l, alpha * acc + _dot(p.astype(BF16), v)


def _diag_blocks(o, rows_per_group):
    return jnp.concatenate(
        [o[g * rows_per_group:(g + 1) * rows_per_group, g * HEAD_DIM:(g + 1) * HEAD_DIM] for g in range(N_KV)], axis=0)


def _page_spec(layer, n_pool, n_pages, pages_per_step, i):
    return pl.BlockSpec((1, KV_W, PAGE_SIZE),
                        lambda bi, s, pt: (layer * n_pool + pt[bi * n_pages + s * pages_per_step + i], 0, 0))


def _block_sum_kernel(pt_ref, *refs, pages_per_step):
    del pt_ref
    o_ref = refs[pages_per_step]
    step = pl.program_id(1)

    @pl.when(step == 0)
    def _():
        o_ref[...] = jnp.zeros(o_ref.shape, F32)

    lane = lax.broadcasted_iota(jnp.int32, (KV_W, o_ref.shape[2]), 1)
    acc = o_ref[0]
    for i in range(pages_per_step):
        blk = _div_pow2(step * pages_per_step + i, MOBA_BLOCK // PAGE_SIZE)
        acc = acc + jnp.where(lane == blk, jnp.sum(refs[i][0], axis=1, keepdims=True), 0.0)
    o_ref[0] = acc


def moba_block_sums(pool, layer, page_table, pages_per_step, n_lanes):
    b, n_pages = page_table.shape
    n_pool = pool.shape[0] // DEPTH
    assert n_pages % pages_per_step == 0 and n_pages * PAGE_SIZE // MOBA_BLOCK <= n_lanes
    return pl.pallas_call(
        functools.partial(_block_sum_kernel, pages_per_step=pages_per_step),
        grid_spec=pltpu.PrefetchScalarGridSpec(
            num_scalar_prefetch=1,
            grid=(b, n_pages // pages_per_step),
            in_specs=[_page_spec(layer, n_pool, n_pages, pages_per_step, i) for i in range(pages_per_step)],
            out_specs=pl.BlockSpec((1, KV_W, n_lanes), lambda bi, s, pt: (bi, 0, 0)),
        ),
        out_shape=jax.ShapeDtypeStruct((b, KV_W, n_lanes), F32),
        compiler_params=_cparams("parallel", "arbitrary"),
        name="moba_block_sums",
    )(page_table.reshape(-1), *([pool] * pages_per_step))


def _cmp_pages_kernel(pt_ref, *refs, n_pages):
    del pt_ref
    page_refs = refs[:n_pages]
    pos_ref, w1_ref, w2_ref, o_ref, buf_ref = refs[n_pages:]
    blocks_per_page = PAGE_SIZE // CMP_BLOCK
    hid = w2_ref.shape[0]
    for i in range(n_pages):
        buf_ref[i * KV_W:(i + 1) * KV_W, :] = page_refs[i][0]
    w2 = w2_ref[...]
    for g in range(N_KV):

        def body(d, acc, g=g):
            x = buf_ref[pl.ds(g * HEAD_DIM + d, n_pages, stride=KV_W), :]
            x = (x + pos_ref[pl.ds(d, 1), :]).astype(BF16)
            x = jnp.concatenate([x[:, i * CMP_BLOCK:(i + 1) * CMP_BLOCK] for i in range(blocks_per_page)], axis=0)
            return acc + _dot(x, w1_ref[d])

        h = lax.fori_loop(0, HEAD_DIM, body, jnp.zeros((blocks_per_page * n_pages, hid), F32), unroll=16)
        h = (h * jax.nn.sigmoid(h)).astype(BF16)
        y = _dot(h, w2)
        for blk in range(blocks_per_page):
            o_ref[0, blk, :, g * HEAD_DIM:(g + 1) * HEAD_DIM] = y[blk * n_pages:(blk + 1) * n_pages, :]


def cmp_pages(pool, layer, page_table, pos_t, w1_by_dim, w2):
    b, n_pages = page_table.shape
    n_pool = pool.shape[0] // DEPTH
    blocks_per_page = PAGE_SIZE // CMP_BLOCK
    page_spec = lambda i: pl.BlockSpec((1, KV_W, PAGE_SIZE), lambda bi, pt: (layer * n_pool + pt[bi * n_pages + i], 0, 0))
    full = lambda a: pl.BlockSpec(a.shape, lambda bi, pt: (0,) * a.ndim)
    return pl.pallas_call(
        functools.partial(_cmp_pages_kernel, n_pages=n_pages),
        grid_spec=pltpu.PrefetchScalarGridSpec(
            num_scalar_prefetch=1,
            grid=(b,),
            in_specs=[page_spec(i) for i in range(n_pages)] + [full(pos_t), full(w1_by_dim), full(w2)],
            out_specs=pl.BlockSpec((1, blocks_per_page, n_pages, KV_W), lambda bi, pt: (bi, 0, 0, 0)),
            scratch_shapes=[pltpu.VMEM((n_pages * KV_W, PAGE_SIZE), F32)],
        ),
        out_shape=jax.ShapeDtypeStruct((b, blocks_per_page, n_pages, KV_W), F32),
        compiler_params=_cparams("parallel"),
        name="cmp_pages",
    )(page_table.reshape(-1), *([pool] * n_pages), pos_t, w1_by_dim, w2)


def _cmp_w1_by_dim(w1):
    return w1.reshape(CMP_BLOCK, HEAD_DIM, w1.shape[1]).transpose(1, 0, 2).astype(BF16)


def _sample_select_kernel(qn_ref, qm_ref, kc_ref, vc_ref, kmean_ref, slope_ref, tpos_ref,
                          ocmp_ref, sbias_ref, mbias_ref, *, past, n_new):
    rows = N_HEADS * n_new
    rpg = N_REP * n_new
    n_cmp = kc_ref.shape[1]
    n_sel_lanes = sbias_ref.shape[2]
    n_moba_lanes = mbias_ref.shape[2]
    slope = slope_ref[...]
    qpos = past + tpos_ref[...]

    blk = lax.broadcasted_iota(jnp.int32, (rows, n_cmp), 1)
    dist = qpos - (blk * CMP_BLOCK + (CMP_BLOCK - 1))
    p = _masked_softmax(_dot_t(qn_ref[0], kc_ref[0].astype(BF16)) - slope * dist.astype(F32), dist >= 0, -1)
    ocmp_ref[0] = _diag_blocks(_dot(p.astype(BF16), vc_ref[0].astype(BF16)), rpg)

    imp = jnp.sum(p.reshape(N_KV, N_REP, n_new, n_cmp), axis=1).reshape(N_KV * n_new, n_cmp)
    imp = jnp.concatenate([imp, jnp.zeros((N_KV * n_new, n_sel_lanes - n_cmp), F32)], axis=1)
    sblk = lax.broadcasted_iota(jnp.int32, (N_KV * n_new, n_sel_lanes), 1)
    assert n_new & (n_new - 1) == 0
    spos = past + (lax.broadcasted_iota(jnp.int32, (N_KV * n_new, 1), 0) & (n_new - 1))
    n_sel_blocks = -(-(past + n_new) // SEL_BLOCK)
    visible = (sblk * SEL_BLOCK + (SEL_BLOCK - 1)) <= spos
    forced = (sblk == _div_pow2(spos, SEL_BLOCK)) | (sblk == 0)
    score = jnp.where(forced, jnp.inf, jnp.where(visible, imp, -jnp.inf))
    score = jnp.where(sblk < n_sel_blocks, score, -jnp.inf)
    sb = _topk_bias(score, N_SEL, sblk, -1)
    sb = jnp.broadcast_to(sb.reshape(N_KV, 1, n_new, n_sel_lanes), (N_KV, N_REP, n_new, n_sel_lanes))
    sbias_ref[0] = sb.reshape(rows, n_sel_lanes)

    mblk = lax.broadcasted_iota(jnp.int32, (rows, n_moba_lanes), 1)
    own = _div_pow2(qpos, MOBA_BLOCK)
    gate = _dot(qm_ref[0], kmean_ref[0].astype(BF16))
    mscore = jnp.where(mblk == own, jnp.inf, jnp.where(mblk < own, gate, -jnp.inf))
    mbias_ref[0] = _topk_bias(mscore, MOBA_TOPK + 1, mblk, -1)


def sample_select(qn, qm, kc, vc, kmean_t, slope, tpos, past, n_new, n_sel_lanes):
    b, rows, _ = qn.shape

    def spec(a):
        return pl.BlockSpec((1,) + a.shape[1:], lambda bi: (bi,) + (0,) * (a.ndim - 1))

    def full(a):
        return pl.BlockSpec(a.shape, lambda bi: (0,) * a.ndim)

    outs = [jax.ShapeDtypeStruct((b, rows, HEAD_DIM), F32),
            jax.ShapeDtypeStruct((b, rows, n_sel_lanes), F32),
            jax.ShapeDtypeStruct((b, rows, kmean_t.shape[2]), F32)]
    return pl.pallas_call(
        functools.partial(_sample_select_kernel, past=past, n_new=n_new),
        grid=(b,),
        in_specs=[spec(qn), spec(qm), spec(kc), spec(vc), spec(kmean_t), full(slope), full(tpos)],
        out_specs=[spec(o) for o in outs],
        out_shape=outs,
        compiler_params=_cparams("parallel"),
        name="sample_select",
    )(qn, qm, kc, vc, kmean_t, slope, tpos)


def _attn_sample_kernel(pt_ref, *refs, pages_per_step, past, n_new):
    del pt_ref
    pps = pages_per_step
    (qn_ref, qm_ref, sbias_ref, mbias_ref, slope_ref, tpos_ref) = refs[:6]
    page_refs = refs[6:6 + 4 * pps]
    (ksn_ref, vsn_ref, kmn_ref, vmn_ref, kw_ref, vw_ref, ocmp_ref, gt_ref) = refs[6 + 4 * pps:14 + 4 * pps]
    on_ref, om_ref = refs[14 + 4 * pps:16 + 4 * pps]
    ms_ref, ls_ref, as_ref, mm_ref, lm_ref, am_ref = refs[16 + 4 * pps:]
    step = pl.program_id(1)
    rpg = N_REP * n_new
    qn = qn_ref[0]
    qm = qm_ref[0]
    slope = slope_ref[...]
    tpos = tpos_ref[...]

    @pl.when(step == 0)
    def _():
        ms_ref[...] = jnp.full(ms_ref.shape, NEG_BIG, F32)
        mm_ref[...] = jnp.full(mm_ref.shape, NEG_BIG, F32)
        ls_ref[...] = jnp.zeros(ls_ref.shape, F32)
        lm_ref[...] = jnp.zeros(lm_ref.shape, F32)
        as_ref[...] = jnp.zeros(as_ref.shape, F32)
        am_ref[...] = jnp.zeros(am_ref.shape, F32)

    sbias = sbias_ref[0].astype(BF16)
    mbias = mbias_ref[0].astype(BF16)
    n_keys = pps * PAGE_SIZE
    first_key = step * n_keys
    key = lax.broadcasted_iota(jnp.int32, (1, n_keys), 1)
    alibi = slope * (first_key - past + key).astype(F32)
    s_blk = _div_pow2(first_key + lax.broadcasted_iota(jnp.int32, (sbias.shape[1], n_keys), 1), SEL_BLOCK)
    s_expand = (lax.broadcasted_iota(jnp.int32, (sbias.shape[1], n_keys), 0) == s_blk).astype(BF16)
    m_blk = _div_pow2(first_key + lax.broadcasted_iota(jnp.int32, (mbias.shape[1], n_keys), 1), MOBA_BLOCK)
    m_expand = (lax.broadcasted_iota(jnp.int32, (mbias.shape[1], n_keys), 0) == m_blk).astype(BF16)

    def update(carry, q, k_refs, v_refs, bias):
        m, l, acc = carry
        s = jnp.concatenate([_dot(q, k_ref[0].astype(BF16)) for k_ref in k_refs], axis=1) + bias
        m_new = jnp.maximum(m, jnp.max(s, axis=-1, keepdims=True))
        alpha = jnp.exp(m - m_new)
        p = jnp.exp(s - m_new)
        l = alpha * l + jnp.sum(p, axis=-1, keepdims=True)
        acc = alpha * acc
        for i, v_ref in enumerate(v_refs):
            acc = acc + _dot_t(p[:, i * PAGE_SIZE:(i + 1) * PAGE_SIZE].astype(BF16), v_ref[0].astype(BF16))
        return m_new, l, acc

    slc = update((ms_ref[...], ls_ref[...], as_ref[...]), qn, page_refs[:pps], page_refs[pps:2 * pps],
                 alibi + _dot(sbias, s_expand))
    moba = update((mm_ref[...], lm_ref[...], am_ref[...]), qm, page_refs[2 * pps:3 * pps], page_refs[3 * pps:],
                  alibi + _dot(mbias, m_expand))
    ms_ref[...], ls_ref[...], as_ref[...] = slc
    mm_ref[...], lm_ref[...], am_ref[...] = moba

    @pl.when(step == pl.num_programs(1) - 1)
    def _():
        new = lax.broadcasted_iota(jnp.int32, (1, n_new), 1)
        bias = jnp.where(new <= tpos, slope * new.astype(F32), NEG_BIG)
        _, l, acc = _flash_update(slc, qn, ksn_ref[0].astype(BF16), vsn_ref[0].astype(BF16), bias)
        o_slc = _diag_blocks(acc / l, rpg)
        _, l, acc = _flash_update(moba, qm, kmn_ref[0].astype(BF16), vmn_ref[0].astype(BF16), bias)
        om_ref[0] = _diag_blocks(acc / l, rpg)

        n_ctx = kw_ref.shape[2]
        wrel = lax.broadcasted_iota(jnp.int32, (1, n_ctx), 1) - (n_ctx - n_new)
        dist = tpos - wrel
        valid = (dist >= 0) & (dist < WINDOW) & (wrel + past >= 0)
        s = jnp.where(valid, _dot(qn, kw_ref[0].astype(BF16)) + slope * wrel.astype(F32), NEG_BIG)
        pw = jnp.exp(s - jnp.max(s, axis=-1, keepdims=True))
        o_win = _diag_blocks(_dot_t(pw.astype(BF16), vw_ref[0].astype(BF16)) / jnp.sum(pw, axis=-1, keepdims=True), rpg)

        gates = jax.nn.sigmoid(gt_ref[0])
        on_ref[0] = gates[:, 0:1] * ocmp_ref[0] + gates[:, 1:2] * o_slc + gates[:, 2:3] * o_win


def attn_sample(page_table, layer, qn, qm, sbias, mbias, slope, tpos, pools, new_kv, kwin_t, vwin_t, ocmp, gt,
                past, n_new, pages_per_step):
    b, rows, _ = qn.shape
    n_pages = page_table.shape[1]
    n_pool = pools[0].shape[0] // DEPTH
    pps = pages_per_step
    assert n_pages % pps == 0 and n_pages * PAGE_SIZE == past

    def spec(a):
        return pl.BlockSpec((1,) + a.shape[1:], lambda bi, s, pt: (bi,) + (0,) * (a.ndim - 1))

    def full(a):
        return pl.BlockSpec(a.shape, lambda bi, s, pt: (0,) * a.ndim)

    page_specs = [_page_spec(layer, n_pool, n_pages, pps, i) for _ in range(4) for i in range(pps)]
    page_args = [pool for pool in pools for _ in range(pps)]
    o_shape = jax.ShapeDtypeStruct((b, rows, HEAD_DIM), F32)
    return pl.pallas_call(
        functools.partial(_attn_sample_kernel, pages_per_step=pps, past=past, n_new=n_new),
        grid_spec=pltpu.PrefetchScalarGridSpec(
            num_scalar_prefetch=1,
            grid=(b, n_pages // pps),
            in_specs=[spec(qn), spec(qm), spec(sbias), spec(mbias), full(slope), full(tpos)] + page_specs
                     + [spec(a) for a in new_kv] + [spec(kwin_t), spec(vwin_t), spec(ocmp), spec(gt)],
            out_specs=[spec(o_shape), spec(o_shape)],
            scratch_shapes=[pltpu.VMEM((rows, 1), F32), pltpu.VMEM((rows, 1), F32), pltpu.VMEM((rows, KV_W), F32),
                            pltpu.VMEM((rows, 1), F32), pltpu.VMEM((rows, 1), F32), pltpu.VMEM((rows, KV_W), F32)],
        ),
        out_shape=[o_shape, o_shape],
        compiler_params=_cparams("parallel", "arbitrary"),
        name="attn_sample",
    )(page_table.reshape(-1), qn, qm, sbias, mbias, slope, tpos, *page_args, *new_kv, kwin_t, vwin_t, ocmp, gt)


def _out_proj(o_nsa, o_moba, gn_ref, gm_ref, wn_ref, wm_ref, x_ref, g_ref, b_ref, o_ref):
    hn = _rms_norm(o_nsa, gn_ref[...]).astype(BF16)
    hm = _rms_norm(o_moba, gm_ref[...]).astype(BF16)
    y = _dot(hn, wn_ref[...]) + _dot(hm, wm_ref[...])
    o_ref[...] = _layer_norm(DEEPNORM_ALPHA * x_ref[...] + y, g_ref[...], b_ref[...])


def _out_proj_kernel(on_ref, om_ref, *refs):
    _out_proj(on_ref[...], om_ref[...], *refs)


def _out_proj_fm_kernel(on_ref, om_ref, *refs):
    _out_proj(on_ref[0].T, om_ref[0].T, *refs)


def out_proj_ln(o_nsa, o_moba, gn_nsa, gn_moba, w_nsa, w_moba, x, g, b, feature_major):
    m = x.shape[0]
    assert m % ROW_TILE == 0
    row = lambda w: pl.BlockSpec((ROW_TILE, w), lambda i: (i, 0))
    full = lambda a: pl.BlockSpec(a.shape, lambda i: (0,) * a.ndim)
    if feature_major:
        per_batch = o_nsa.shape[2] // ROW_TILE
        o_spec = pl.BlockSpec((1, Q_W, ROW_TILE), lambda i: (i // per_batch, 0, i % per_batch))
    else:
        o_spec = row(Q_W)
    return pl.pallas_call(
        _out_proj_fm_kernel if feature_major else _out_proj_kernel,
        grid=(m // ROW_TILE,),
        in_specs=[o_spec, o_spec, full(gn_nsa), full(gn_moba), full(w_nsa), full(w_moba),
                  row(D_MODEL), full(g), full(b)],
        out_specs=row(D_MODEL),
        out_shape=jax.ShapeDtypeStruct((m, D_MODEL), F32),
        compiler_params=_cparams("parallel"),
        name="out_proj_ln",
    )(o_nsa, o_moba, gn_nsa, gn_moba, w_nsa, w_moba, x, g, b)


def _mem_attn_kernel(x_ref, wq_ref, k_ref, v_ref, wo_ref, g_ref, b_ref, o_ref):
    x = x_ref[0]
    q = _dot(x.astype(BF16), wq_ref[...]).astype(BF16)
    k = k_ref[0].astype(BF16)
    v = v_ref[0].astype(BF16)
    heads = []
    for h in range(MEM_HEADS):
        sl = slice(h * MEM_HEAD_DIM, (h + 1) * MEM_HEAD_DIM)
        s = _dot_t(q[:, sl], k[:, sl]) * (MEM_HEAD_DIM ** -0.5)
        e = jnp.exp(s - jnp.max(s, axis=-1, keepdims=True))
        p = e / jnp.sum(e, axis=-1, keepdims=True)
        heads.append(_dot(p.astype(BF16), v[:, sl]))
    o = jnp.concatenate(heads, axis=-1).astype(BF16)
    o_ref[0] = _layer_norm(DEEPNORM_ALPHA * x + _dot(o, wo_ref[...]), g_ref[...], b_ref[...])


def mem_attn_ln(x, wq, mem_k, mem_v, wo, g, b, tq):
    bsz, t, d = x.shape
    n_mem, d_mem = mem_k.shape[1:]
    assert t % tq == 0
    full = lambda a: pl.BlockSpec(a.shape, lambda bi, i: (0,) * a.ndim)
    return pl.pallas_call(
        _mem_attn_kernel,
        grid=(bsz, t // tq),
        in_specs=[pl.BlockSpec((1, tq, d), lambda bi, i: (bi, i, 0)), full(wq),
                  pl.BlockSpec((1, n_mem, d_mem), lambda bi, i: (bi, 0, 0)),
                  pl.BlockSpec((1, n_mem, d_mem), lambda bi, i: (bi, 0, 0)), full(wo), full(g), full(b)],
        out_specs=pl.BlockSpec((1, tq, d), lambda bi, i: (bi, i, 0)),
        out_shape=jax.ShapeDtypeStruct(x.shape, F32),
        compiler_params=_cparams("parallel", "arbitrary"),
        name="mem_attn_ln",
    )(x, wq, mem_k, mem_v, wo, g, b)


def _router_kernel(x_ref, w_ref, b_ref, e_ref, g_ref):
    logits = jnp.dot(x_ref[...], w_ref[...], preferred_element_type=F32, precision=lax.Precision.HIGHEST) + b_ref[...]
    lane = lax.broadcasted_iota(jnp.int32, logits.shape, 1).astype(F32)
    cur = logits
    vals, idxs = [], []
    for _ in range(TOP_K):
        m = jnp.max(cur, axis=-1, keepdims=True)
        idx = jnp.min(jnp.where(cur == m, lane, float(N_EXPERTS)), axis=-1, keepdims=True)
        vals.append(m)
        idxs.append(idx)
        cur = jnp.where(lane == idx, -jnp.inf, cur)
    top_v = jnp.concatenate(vals, axis=-1)
    e = jnp.exp(top_v - vals[0])
    g_ref[...] = e / jnp.sum(e, axis=-1, keepdims=True)
    e_ref[...] = jnp.concatenate(idxs, axis=-1).astype(jnp.int32)


def router(x, w, b):
    m, d = x.shape
    assert m % ROW_TILE == 0
    return pl.pallas_call(
        _router_kernel,
        grid=(m // ROW_TILE,),
        in_specs=[pl.BlockSpec((ROW_TILE, d), lambda i: (i, 0)),
                  pl.BlockSpec(w.shape, lambda i: (0, 0)), pl.BlockSpec(b.shape, lambda i: (0, 0))],
        out_specs=[pl.BlockSpec((ROW_TILE, TOP_K), lambda i: (i, 0)), pl.BlockSpec((ROW_TILE, TOP_K), lambda i: (i, 0))],
        out_shape=[jax.ShapeDtypeStruct((m, TOP_K), jnp.int32), jax.ShapeDtypeStruct((m, TOP_K), F32)],
        compiler_params=_cparams("parallel"),
        name="router",
    )(x, w, b)


def _dispatch_kernel(nu_ref, src_ref, src_next_ref, x_hbm, o_ref, buf_ref, sem_ref):
    i = pl.program_id(0)
    n_used = nu_ref[0]
    slot = i & 1

    def row_copy(row, r, slot):
        return pltpu.make_async_copy(x_hbm.at[pl.ds(row, 1), :], buf_ref.at[slot, pl.ds(r, 1), :], sem_ref.at[slot])

    def start_gather(rows_ref, slot):
        def body(r, carry):
            row_copy(rows_ref[0, 0, r], r, slot).start()
            return carry

        lax.fori_loop(0, MOE_TILE, body, 0, unroll=8)

    @pl.when((i == 0) & (n_used > 0))
    def _():
        start_gather(src_ref, 0)

    @pl.when(i + 1 < n_used)
    def _():
        start_gather(src_next_ref, 1 - slot)

    @pl.when(i < n_used)
    def _():
        def body(r, carry):
            row_copy(0, r, slot).wait()
            return carry

        lax.fori_loop(0, MOE_TILE, body, 0, unroll=8)
        o_ref[...] = buf_ref[slot].astype(BF16)

    @pl.when(i >= n_used)
    def _():
        o_ref[...] = jnp.zeros(o_ref.shape, BF16)


def dispatch_rows(x, src, n_used):
    n, d = x.shape
    n_blocks = src.shape[0] // MOE_TILE
    src3 = src.reshape(n_blocks, 1, MOE_TILE)
    return pl.pallas_call(
        _dispatch_kernel,
        grid_spec=pltpu.PrefetchScalarGridSpec(
            num_scalar_prefetch=1,
            grid=(n_blocks,),
            in_specs=[pl.BlockSpec((1, 1, MOE_TILE), lambda i, nu: (i, 0, 0), memory_space=pltpu.SMEM),
                      pl.BlockSpec((1, 1, MOE_TILE), lambda i, nu: (jnp.minimum(i + 1, n_blocks - 1), 0, 0),
                                   memory_space=pltpu.SMEM),
                      pl.BlockSpec(memory_space=pl.ANY)],
            out_specs=pl.BlockSpec((MOE_TILE, d), lambda i, nu: (i, 0)),
            scratch_shapes=[pltpu.VMEM((2, MOE_TILE, d), F32), pltpu.SemaphoreType.DMA((2,))],
        ),
        out_shape=jax.ShapeDtypeStruct((n_blocks * MOE_TILE, d), BF16),
        compiler_params=_cparams("arbitrary"),
        name="dispatch_rows",
    )(n_used, src3, src3, x)


def _expert_up_kernel(be_ref, nu_ref, x_ref, wg_ref, wu_ref, bg_ref, bu_ref, h_ref):
    del be_ref

    @pl.when(pl.program_id(1) < nu_ref[0])
    def _():
        x = x_ref[...]
        g = _dot(x, wg_ref[0].astype(BF16)) + bg_ref[0]
        u = _dot(x, wu_ref[0].astype(BF16)) + bu_ref[0]
        g = jnp.minimum(g, SWIGLU_LIMIT)
        u = jnp.clip(u, -SWIGLU_LIMIT, SWIGLU_LIMIT)
        h_ref[...] = (g * jax.nn.sigmoid(SWIGLU_ALPHA * g) * (u + 1.0)).astype(BF16)

    @pl.when(pl.program_id(1) >= nu_ref[0])
    def _():
        h_ref[...] = jnp.zeros(h_ref.shape, BF16)


def _expert_down_kernel(be_ref, nu_ref, h_ref, wd_ref, bd_ref, y_ref):
    del be_ref

    @pl.when(pl.program_id(1) < nu_ref[0])
    def _():
        y_ref[...] = _dot(h_ref[...], wd_ref[0].astype(BF16)) + bd_ref[0]

    @pl.when(pl.program_id(1) >= nu_ref[0])
    def _():
        y_ref[...] = jnp.zeros(y_ref.shape, F32)


def expert_ffn(xs, blk_e, n_used, layer, wg, bg, wu, bu, wd, bd):
    cap, d = xs.shape
    n_blocks = cap // MOE_TILE
    d_ff = wg.shape[-1]
    we = lambda j, i, be, nu: layer * N_EXPERTS + be[i]
    h = pl.pallas_call(
        _expert_up_kernel,
        grid_spec=pltpu.PrefetchScalarGridSpec(
            num_scalar_prefetch=2,
            grid=(d_ff // FF_CHUNK, n_blocks),
            in_specs=[pl.BlockSpec((MOE_TILE, d), lambda j, i, be, nu: (i, 0)),
                      pl.BlockSpec((1, d, FF_CHUNK), lambda j, i, be, nu: (we(j, i, be, nu), 0, j)),
                      pl.BlockSpec((1, d, FF_CHUNK), lambda j, i, be, nu: (we(j, i, be, nu), 0, j)),
                      pl.BlockSpec((1, 1, FF_CHUNK), lambda j, i, be, nu: (we(j, i, be, nu), 0, j)),
                      pl.BlockSpec((1, 1, FF_CHUNK), lambda j, i, be, nu: (we(j, i, be, nu), 0, j))],
            out_specs=pl.BlockSpec((MOE_TILE, FF_CHUNK), lambda j, i, be, nu: (i, j)),
        ),
        out_shape=jax.ShapeDtypeStruct((cap, d_ff), BF16),
        compiler_params=_cparams("parallel", "arbitrary"),
        name="expert_up",
    )(blk_e, n_used, xs, wg, wu, bg, bu)
    return pl.pallas_call(
        _expert_down_kernel,
        grid_spec=pltpu.PrefetchScalarGridSpec(
            num_scalar_prefetch=2,
            grid=(d // DOWN_CHUNK, n_blocks),
            in_specs=[pl.BlockSpec((MOE_TILE, d_ff), lambda j, i, be, nu: (i, 0)),
                      pl.BlockSpec((1, d_ff, DOWN_CHUNK), lambda j, i, be, nu: (we(j, i, be, nu), 0, j)),
                      pl.BlockSpec((1, 1, DOWN_CHUNK), lambda j, i, be, nu: (we(j, i, be, nu), 0, j))],
            out_specs=pl.BlockSpec((MOE_TILE, DOWN_CHUNK), lambda j, i, be, nu: (i, j)),
        ),
        out_shape=jax.ShapeDtypeStruct((cap, d), F32),
        compiler_params=_cparams("parallel", "arbitrary"),
        name="expert_down",
    )(blk_e, n_used, h, wd, bd)


def _residual_ln_kernel(x_ref, m_ref, g_ref, b_ref, o_ref):
    o_ref[...] = _layer_norm(DEEPNORM_ALPHA * x_ref[...] + m_ref[...], g_ref[...], b_ref[...])


def residual_ln(x, m, g, b):
    n, d = x.shape
    row = pl.BlockSpec((ROW_TILE, d), lambda i: (i, 0))
    full = lambda a: pl.BlockSpec(a.shape, lambda i: (0,) * a.ndim)
    return pl.pallas_call(
        _residual_ln_kernel,
        grid=(n // ROW_TILE,),
        in_specs=[row, row, full(g), full(b)],
        out_specs=row,
        out_shape=jax.ShapeDtypeStruct((n, d), F32),
        compiler_params=_cparams("parallel"),
        name="residual_ln",
    )(x, m, g, b)


def moe_ffn(x, layer, router_w, router_b, wg, bg, wu, bu, wd, bd):
    n, d = x.shape
    top_e, gate = router(x, router_w, router_b)
    a = n * TOP_K
    e_flat = top_e.reshape(a)
    onehot = (e_flat[:, None] == jnp.arange(N_EXPERTS, dtype=jnp.int32)[None, :]).astype(jnp.int32)
    csum = jnp.cumsum(onehot, axis=0)
    rank = jnp.sum((csum - onehot) * onehot, axis=1)
    counts = csum[-1]
    padded = (counts + MOE_TILE - 1) // MOE_TILE * MOE_TILE
    pad_end = jnp.cumsum(padded)
    pad_start = pad_end - padded
    dest = pad_start[e_flat] + rank
    n_blocks = -(-a // MOE_TILE) + N_EXPERTS
    cap = n_blocks * MOE_TILE
    tok = jnp.arange(a, dtype=jnp.int32) // TOP_K
    src = jnp.zeros((cap,), jnp.int32).at[dest].set(tok)
    blk_start = jnp.arange(n_blocks, dtype=jnp.int32) * MOE_TILE
    blk_e = jnp.minimum(jnp.sum(blk_start[:, None] >= pad_end[None, :], axis=1), N_EXPERTS - 1).astype(jnp.int32)
    n_used = (pad_end[-1] // MOE_TILE).astype(jnp.int32).reshape(1)
    xs = dispatch_rows(x, src, n_used)
    y = expert_ffn(xs, blk_e, n_used, layer, wg, bg, wu, bu, wd, bd)
    return jnp.sum(y[dest.reshape(n, TOP_K).T] * gate.T[:, :, None], axis=0)


def _alibi_slopes():
    return jnp.asarray(2.0 ** (-8.0 * np.arange(1, N_HEADS + 1) / N_HEADS), F32)


def _split_w_in(w):
    q_n, kv_n, gt, q_m, kv_m = jnp.split(w, [Q_W, Q_W + 6 * KV_W, Q_W + 6 * KV_W + 3 * N_HEADS,
                                            2 * Q_W + 6 * KV_W + 3 * N_HEADS], axis=1)
    kc, vc, ks, vs, kw, vw = jnp.split(kv_n, 6, axis=1)
    km, vm = jnp.split(kv_m, 2, axis=1)
    return q_n, (kc, vc, ks, vs, kw, vw, km, vm), gt, q_m


def _block_diag_rows(q, b, t):
    q = (q * (HEAD_DIM ** -0.5)).astype(BF16).reshape(b, t, N_KV, N_REP, HEAD_DIM).transpose(0, 2, 3, 1, 4)
    eye = jnp.eye(N_KV, dtype=BF16)
    bd = q[:, :, :, :, None, :] * eye[None, :, None, None, :, None]
    return bd.reshape(b, N_HEADS * t, KV_W)


def _rows_to_tokens(o, b, t):
    return o.reshape(b, N_KV, N_REP, t, HEAD_DIM).transpose(0, 3, 1, 2, 4).reshape(b * t, Q_W)


def _token_minor(a):
    lead = a.shape[:-2]
    a = a.reshape(lead + (N_KV, HEAD_DIM, a.shape[-1]))
    n = len(lead)
    return a.transpose(tuple(range(n)) + (n + 2, n, n + 1))


def kernel(x_prompt, x_sample, cache_nsa_cmp_k, cache_nsa_cmp_v, cache_nsa_slc_k, cache_nsa_slc_v, cache_moba_k, cache_moba_v, cache_nsa_win_k, cache_nsa_win_v, cache_mem_k, cache_mem_v, page_table, mem_prompt, w_in, nsa_cmp_pos, nsa_cmp_k_w1, nsa_cmp_k_w2, nsa_cmp_v_w1, nsa_cmp_v_w2, gn_nsa, gn_moba, w_out, ln1_g, ln1_b, ca_wq, ca_wk, ca_wv, ca_wo, ln2_g, ln2_b, router_w, router_b, exp_wg, exp_bg, exp_wu, exp_bu, exp_wd, exp_bd, ln3_g, ln3_b):
    bp, tp, d = x_prompt.shape
    bs, ts, _ = x_sample.shape
    n_p, n_s = bp * tp, bs * ts
    n_mem = mem_prompt.shape[1]
    depth, n_pool = cache_nsa_cmp_k.shape[:2]
    n_pages = page_table.shape[1]
    past = n_pages * PAGE_SIZE
    wbuf = cache_nsa_win_k.shape[2]
    d_mem = MEM_HEADS * MEM_HEAD_DIM
    slopes = _alibi_slopes()
    fm = lambda c: c.transpose(0, 1, 3, 4, 2).reshape(c.shape[0], c.shape[1], KV_W, c.shape[2])
    pool = lambda c: fm(c).reshape(depth * n_pool, KV_W, PAGE_SIZE)
    pools = tuple(map(pool, (cache_nsa_slc_k, cache_nsa_slc_v, cache_moba_k, cache_moba_v)))
    cmp_pools = tuple(map(pool, (cache_nsa_cmp_k, cache_nsa_cmp_v)))
    win_k_fm, win_v_fm = fm(cache_nsa_win_k), fm(cache_nsa_win_v)
    wg = exp_wg.reshape((depth * N_EXPERTS,) + exp_wg.shape[2:])
    wu = exp_wu.reshape((depth * N_EXPERTS,) + exp_wu.shape[2:])
    wd = exp_wd.reshape((depth * N_EXPERTS,) + exp_wd.shape[2:])
    bg = exp_bg.reshape(depth * N_EXPERTS, 1, -1)
    bu = exp_bu.reshape(depth * N_EXPERTS, 1, -1)
    bd = exp_bd.reshape(depth * N_EXPERTS, 1, -1)

    row = np.arange(N_HEADS * ts)
    s_slope = slopes[row // ts].reshape(-1, 1)
    s_tpos = jnp.asarray((row % ts).reshape(-1, 1), jnp.int32)
    n_cmp_s = (past + ts) // CMP_BLOCK
    n_sel_lanes = -(-(-(-(past + ts) // SEL_BLOCK)) // 128) * 128
    n_moba_past = past // MOBA_BLOCK

    x_p = x_prompt
    x_s = x_sample.reshape(n_s, d)
    p_states, s_states, mem_states = [], [], []
    s_win = []
    for l in range(depth):
        q_n, kv, gt, q_m = _split_w_in(w_in[l])
        kc, vc, ks, vs, kw, vw, km, vm = kv
        pos = nsa_cmp_pos[l].reshape(1, CMP_BLOCK * HEAD_DIM)
        ck_w1, ck_w2 = nsa_cmp_k_w1[l].astype(BF16), nsa_cmp_k_w2[l].astype(BF16)
        cv_w1, cv_w2 = nsa_cmp_v_w1[l].astype(BF16), nsa_cmp_v_w2[l].astype(BF16)

        wq_t = jnp.concatenate([q_n, q_m], axis=1).T.astype(BF16)
        wkv_t = jnp.concatenate([kc, vc, ks, vs, km, vm, kw, vw], axis=1).T.astype(BF16)
        wgt_t = jnp.pad(gt.reshape(d, N_KV, 3 * N_REP), ((0, 0), (0, 0), (0, GATE_PAD - 3 * N_REP)))
        wgt_t = wgt_t.reshape(d, N_KV * GATE_PAD).T.astype(BF16)
        wk_tm = jnp.concatenate([kc, vc, ks, kw, km], axis=1).astype(BF16)
        (q_fm,) = proj_feature_major(x_p, wq_t, (BF16,), HEAD_DIM ** -0.5)
        st_fm, kv_fm = proj_feature_major(x_p, wkv_t, (F32, BF16))
        (gt_fm,) = proj_feature_major(x_p, wgt_t, (F32,))
        kc_tm, vc_tm, k_tm = proj_keys_token_major(x_p, wk_tm)
        n_cmp_p = tp // CMP_BLOCK
        k_cmp = cmp_mlp(kc_tm.reshape(bp * N_KV * n_cmp_p, CMP_BLOCK * HEAD_DIM), pos, ck_w1, ck_w2, 256)
        v_cmp = cmp_mlp(vc_tm.reshape(bp * N_KV * n_cmp_p, CMP_BLOCK * HEAD_DIM), pos, cv_w1, cv_w2, 256)
        k_cmp = k_cmp.reshape(bp, N_KV, n_cmp_p, HEAD_DIM)
        v_cmp_t = v_cmp.reshape(bp, N_KV, n_cmp_p, HEAD_DIM).transpose(0, 1, 3, 2)
        o_nsa_p, o_moba_p = attn_prompt(slopes, q_fm, k_tm, kv_fm, k_cmp, v_cmp_t, gt_fm)

        w_main = jnp.concatenate([q_n, kc, vc, ks, vs, kw, vw, q_m, km, vm], axis=1).astype(BF16)
        w_gate = jnp.pad(gt, ((0, 0), (0, 128 - 3 * N_HEADS))).astype(BF16)
        proj_s = matmul(x_s, w_main, 256, 512)
        gt_s = matmul(x_s, w_gate, 256, 128)[:, :3 * N_HEADS]
        qs_n, kc_s, vc_s, ks_s, vs_s, kw_s, vw_s, qs_m, km_s, vm_s = jnp.split(
            proj_s, [Q_W + KV_W * i for i in range(7)] + [2 * Q_W + 6 * KV_W, 2 * Q_W + 7 * KV_W], axis=1)
        sn = lambda a: a.reshape(bs, ts, KV_W)

        pos_t = jnp.tile(nsa_cmp_pos[l].T, (1, PAGE_SIZE // CMP_BLOCK))
        blocks = lambda a: a.transpose(0, 2, 1, 3).reshape(bs, n_cmp_s, KV_W)
        k_cmp_s = blocks(cmp_pages(cmp_pools[0], l, page_table, pos_t, _cmp_w1_by_dim(nsa_cmp_k_w1[l]), ck_w2))
        v_cmp_s = blocks(cmp_pages(cmp_pools[1], l, page_table, pos_t, _cmp_w1_by_dim(nsa_cmp_v_w1[l]), cv_w2))
        k_mean_t = moba_block_sums(pools[2], l, page_table, 8, 128)
        k_mean_t = k_mean_t.at[:, :, n_moba_past].set(sn(km_s).sum(axis=1)) * (1.0 / MOBA_BLOCK)
        qn_s = _block_diag_rows(qs_n, bs, ts)
        qm_s = _block_diag_rows(qs_m, bs, ts)
        o_cmp_s, sbias, mbias = sample_select(qn_s, qm_s, k_cmp_s, v_cmp_s, k_mean_t, s_slope, s_tpos, past, ts, n_sel_lanes)
        kwin_t = jnp.concatenate([win_k_fm[l], sn(kw_s).transpose(0, 2, 1)], axis=2)
        vwin_t = jnp.concatenate([win_v_fm[l], sn(vw_s).transpose(0, 2, 1)], axis=2)
        gt_rows = gt_s.reshape(bs, ts, N_KV, N_REP, 3).transpose(0, 2, 3, 1, 4).reshape(bs, N_HEADS * ts, 3)
        o_nsa_s, o_moba_s = attn_sample(
            page_table, l, qn_s, qm_s, sbias, mbias, s_slope, s_tpos, pools, (sn(ks_s), sn(vs_s), sn(km_s), sn(vm_s)),
            kwin_t, vwin_t, o_cmp_s, gt_rows, past, ts, 8)

        w_o = w_out[l].astype(BF16)
        op_args = (gn_nsa[l].reshape(1, -1), gn_moba[l].reshape(1, -1), w_o[:Q_W], w_o[Q_W:])
        ln1 = (ln1_g[l].reshape(1, -1), ln1_b[l].reshape(1, -1))
        x_p = out_proj_ln(o_nsa_p, o_moba_p, *op_args, x_p.reshape(n_p, d), *ln1, True)
        x_s = out_proj_ln(_rows_to_tokens(o_nsa_s, bs, ts), _rows_to_tokens(o_moba_s, bs, ts), *op_args, x_s, *ln1, False)

        mem_kv = matmul(mem_prompt.reshape(bp * n_mem, d), jnp.concatenate([ca_wk[l], ca_wv[l]], axis=1).astype(BF16), 512, 512)
        mem_k, mem_v = mem_kv[:, :d_mem].reshape(bp, n_mem, d_mem), mem_kv[:, d_mem:].reshape(bp, n_mem, d_mem)
        wq, wo = ca_wq[l].astype(BF16), ca_wo[l].astype(BF16)
        g2, b2 = ln2_g[l].reshape(1, -1), ln2_b[l].reshape(1, -1)
        x_p = mem_attn_ln(x_p.reshape(bp, tp, d), wq, mem_k, mem_v, wo, g2, b2, ROW_TILE)
        x_s = mem_attn_ln(x_s.reshape(bs, ts, d), wq, cache_mem_k[l].reshape(bs, -1, d_mem),
                          cache_mem_v[l].reshape(bs, -1, d_mem), wo, g2, b2, ts)

        x = jnp.concatenate([x_p.reshape(n_p, d), x_s.reshape(n_s, d)], axis=0)
        m = moe_ffn(x, l, router_w[l], router_b[l].reshape(1, -1), wg, bg, wu, bu, wd, bd)
        x = residual_ln(x, m, ln3_g[l].reshape(1, -1), ln3_b[l].reshape(1, -1))
        x_p, x_s = x[:n_p].reshape(bp, tp, d), x[n_p:]

        p_states.append(st_fm)
        mem_states.append((mem_k.reshape(bp, n_mem, MEM_HEADS, MEM_HEAD_DIM), mem_v.reshape(bp, n_mem, MEM_HEADS, MEM_HEAD_DIM)))
        s_states.append([a.reshape(bs, ts, N_KV, HEAD_DIM) for a in (kc_s, vc_s, ks_s, vs_s, km_s, vm_s)])
        keep_s = min(WINDOW, wbuf + ts)
        s_win.append((kwin_t[:, :, -keep_s:], vwin_t[:, :, -keep_s:]))

    keep = min(WINDOW, tp)
    st = jnp.stack(p_states)
    p_out = [_token_minor(st[:, :, i * KV_W:(i + 1) * KV_W, :]) for i in range(6)]
    p_out += [_token_minor(st[:, :, i * KV_W:(i + 1) * KV_W, tp - keep:]) for i in (6, 7)]
    p_out += [jnp.stack([ms[i] for ms in mem_states]) for i in range(2)]
    s_out = [jnp.stack([ss[i] for ss in s_states]) for i in range(6)]
    s_out += [_token_minor(jnp.stack([w[i] for w in s_win])) for i in range(2)]
    return (x_p, x_s.reshape(bs, ts, d)) + tuple(p_out) + tuple(s_out)
```

```python
import functools

import numpy as np
import jax
import jax.numpy as jnp
from jax import lax
from jax.experimental import pallas as pl
from jax.experimental.pallas import tpu as pltpu

F32 = jnp.float32
BF16 = jnp.bfloat16

D_MODEL = 2048
HEAD_DIM = 64
N_KV = 4
N_REP = 4
N_HEADS = N_KV * N_REP
CMP_BLOCK = 64
SEL_BLOCK = 64
N_SEL = 8
WINDOW = 512
MOBA_BLOCK = 256
MOBA_TOPK = 3
MEM_HEADS = 4
MEM_HEAD_DIM = 128
N_EXPERTS = 32
TOP_K = 4
SWIGLU_LIMIT = 7.0
SWIGLU_ALPHA = 1.702
LN_EPS = 1e-5
NEG_BIG = -1e30
DEPTH = 2
DEEPNORM_ALPHA = (2 * DEPTH) ** 0.25
PAGE_SIZE = 128

KV_W = N_KV * HEAD_DIM
Q_W = N_HEADS * HEAD_DIM
N_STATE = 8
GATE_PAD = 16
ALIBI_PARTS = 3
BLOCK_ROW0 = 16
TQ = 256
VMEM_LIMIT = 56 * 1024 * 1024
ROW_TILE = 256
PROJ_TILE = 512
SAMPLE_PAGES_PER_STEP = 16
CMP_DIMS_PER_DOT = 4
MOE_TILE = 512
FF_CHUNK = 1024
DOWN_CHUNK = 1024


def _cparams(*sem):
    return pltpu.CompilerParams(dimension_semantics=sem, vmem_limit_bytes=VMEM_LIMIT)


def _dot(a, b):
    return jnp.dot(a, b, preferred_element_type=F32)


def _dot_t(a, b):
    return lax.dot_general(a, b, (((1,), (1,)), ((), ())), preferred_element_type=F32)


def _div_pow2(x, n):
    assert n & (n - 1) == 0
    return lax.shift_right_logical(x, n.bit_length() - 1)


def _layer_norm(z, g, b):
    mu = jnp.mean(z, axis=-1, keepdims=True)
    zc = z - mu
    var = jnp.mean(zc * zc, axis=-1, keepdims=True)
    return zc * lax.rsqrt(var + LN_EPS) * g + b


def _rms_norm(x, g):
    return x * lax.rsqrt(jnp.mean(x * x, axis=-1, keepdims=True) + LN_EPS) * g


def _masked_softmax(s, mask, axis):
    s = jnp.where(mask, s, NEG_BIG)
    m = jnp.max(s, axis=axis, keepdims=True)
    e = jnp.where(mask, jnp.exp(s - m), 0.0)
    den = jnp.sum(e, axis=axis, keepdims=True)
    return e / jnp.where(den > 0.0, den, 1.0)


def _topk_bias(score, k, index, axis):
    index = index.astype(F32)
    cur = score
    bias = jnp.full(score.shape, NEG_BIG, F32)
    for _ in range(k):
        m = jnp.max(cur, axis=axis, keepdims=True)
        first = jnp.min(jnp.where(cur == m, index, float(score.shape[axis])), axis=axis, keepdims=True)
        pick = index == first
        bias = jnp.where(pick & (m > -jnp.inf), 0.0, bias)
        cur = jnp.where(pick, -jnp.inf, cur)
    return bias


def _mm_kernel(x_ref, w_ref, o_ref):
    o_ref[...] = _dot(x_ref[...].astype(BF16), w_ref[...])


def matmul(x, w, tm, tn):
    m, k = x.shape
    n = w.shape[1]
    tm = max(t for t in range(8, min(tm, m) + 1, 8) if m % t == 0)
    assert n % tn == 0
    return pl.pallas_call(
        _mm_kernel,
        grid=(m // tm, n // tn),
        in_specs=[pl.BlockSpec((tm, k), lambda i, j: (i, 0)),
                  pl.BlockSpec((k, tn), lambda i, j: (0, j))],
        out_specs=pl.BlockSpec((tm, tn), lambda i, j: (i, j)),
        out_shape=jax.ShapeDtypeStruct((m, n), F32),
        compiler_params=_cparams("parallel", "arbitrary"),
        name="matmul",
    )(x, w)


def _proj_fm_kernel(x_ref, w_ref, *o_refs, scale):
    r = _dot_t(w_ref[...], x_ref[0].astype(BF16))
    if scale != 1.0:
        r = r * scale
    for o_ref in o_refs:
        o_ref[0] = r.astype(o_ref.dtype)


def proj_feature_major(x, w_t, dtypes, scale=1.0):
    b, t, k = x.shape
    f = w_t.shape[0]
    tf = min(PROJ_TILE, f)
    assert t % PROJ_TILE == 0 and f % tf == 0
    o_spec = pl.BlockSpec((1, tf, PROJ_TILE), lambda bi, j, fi: (bi, fi, j))
    return pl.pallas_call(
        functools.partial(_proj_fm_kernel, scale=scale),
        grid=(b, t // PROJ_TILE, f // tf),
        in_specs=[pl.BlockSpec((1, PROJ_TILE, k), lambda bi, j, fi: (bi, j, 0)),
                  pl.BlockSpec((tf, k), lambda bi, j, fi: (fi, 0))],
        out_specs=[o_spec for _ in dtypes],
        out_shape=[jax.ShapeDtypeStruct((b, f, t), dt) for dt in dtypes],
        compiler_params=_cparams("parallel", "parallel", "arbitrary"),
        name="proj_feature_major",
    )(x, w_t)


def _proj_keys_kernel(x_ref, w_ref, kc_ref, vc_ref, k_ref):
    r = _dot(x_ref[0].astype(BF16), w_ref[...])
    for g in range(N_KV):
        lo = g * HEAD_DIM
        kc_ref[0, g] = r[:, lo:lo + HEAD_DIM]
        vc_ref[0, g] = r[:, KV_W + lo:KV_W + lo + HEAD_DIM]
        for w in range(3):
            k_ref[0, w, g] = r[:, (2 + w) * KV_W + lo:(2 + w) * KV_W + lo + HEAD_DIM].astype(BF16)


def proj_keys_token_major(x, w):
    b, t, k = x.shape
    assert t % PROJ_TILE == 0 and w.shape[1] == 5 * KV_W
    c_spec = pl.BlockSpec((1, N_KV, PROJ_TILE, HEAD_DIM), lambda bi, j: (bi, 0, j, 0))
    c_shape = jax.ShapeDtypeStruct((b, N_KV, t, HEAD_DIM), F32)
    return pl.pallas_call(
        _proj_keys_kernel,
        grid=(b, t // PROJ_TILE),
        in_specs=[pl.BlockSpec((1, PROJ_TILE, k), lambda bi, j: (bi, j, 0)),
                  pl.BlockSpec(w.shape, lambda bi, j: (0, 0))],
        out_specs=[c_spec, c_spec, pl.BlockSpec((1, 3, N_KV, PROJ_TILE, HEAD_DIM), lambda bi, j: (bi, 0, 0, j, 0))],
        out_shape=[c_shape, c_shape, jax.ShapeDtypeStruct((b, 3, N_KV, t, HEAD_DIM), BF16)],
        compiler_params=_cparams("parallel", "arbitrary"),
        name="proj_keys_token_major",
    )(x, w)


def _cmp_mlp_kernel(x_ref, pos_ref, w1_ref, w2_ref, o_ref):
    x = (x_ref[...] + pos_ref[...]).astype(BF16)
    h = _dot(x, w1_ref[...])
    h = h * jax.nn.sigmoid(h)
    o_ref[...] = _dot(h.astype(BF16), w2_ref[...])


def cmp_mlp(x, pos, w1, w2, tr):
    r, k = x.shape
    hid = w1.shape[1]
    tr = min(tr, r)
    assert r % tr == 0
    return pl.pallas_call(
        _cmp_mlp_kernel,
        grid=(r // tr,),
        in_specs=[pl.BlockSpec((tr, k), lambda i: (i, 0)),
                  pl.BlockSpec((1, k), lambda i: (0, 0)),
                  pl.BlockSpec((k, hid), lambda i: (0, 0)),
                  pl.BlockSpec((hid, HEAD_DIM), lambda i: (0, 0))],
        out_specs=pl.BlockSpec((tr, HEAD_DIM), lambda i: (i, 0)),
        out_shape=jax.ShapeDtypeStruct((r, HEAD_DIM), F32),
        compiler_params=_cparams("parallel"),
        name="cmp_mlp",
    )(x, pos, w1, w2)


def _flash_t(carry, k_tile, q_aug, vt_tile, mask, shift=None):
    m, l, acc = carry
    s = _dot(k_tile, q_aug)
    if shift is not None:
        s = s + shift
    if mask is not None:
        s = jnp.where(mask, s, NEG_BIG)
    m_new = jnp.maximum(m, jnp.max(s, axis=0, keepdims=True))
    alpha = jnp.exp(m - m_new)
    p = jnp.exp(s - m_new)
    l = alpha * l + jnp.sum(p, axis=0, keepdims=True)
    acc = alpha * acc + _dot(vt_tile, p.astype(BF16))
    return m_new, l, acc


def _attn_prompt_kernel(slopes_ref, qn_ref, qm_ref, ks_ref, kw_ref, km_ref, vs_ref, vw_ref, vm_ref,
                        kc_ref, vct_ref, gt_ref, on_ref, om_ref, kaug_ref, kmean_ref, *, seq):
    g = pl.program_id(1)
    qt = pl.program_id(2)
    n_cmp = seq // CMP_BLOCK
    n_moba = seq // MOBA_BLOCK
    t0 = qt * TQ

    @pl.when(qt == 0)
    def _():
        ipos = lax.broadcasted_iota(jnp.int32, (seq, HEAD_DIM), 0)
        pos = ipos.astype(F32)
        lane = lax.broadcasted_iota(jnp.int32, (seq, HEAD_DIM), 1)
        cols = jnp.zeros((seq, HEAD_DIM), F32)
        for r in range(N_REP):
            v = pos * slopes_ref[g * N_REP + r]
            hi = v.astype(BF16).astype(F32)
            mid = (v - hi).astype(BF16).astype(F32)
            lo = v - hi - mid
            c = ALIBI_PARTS * r
            cols = jnp.where(lane == c, hi, jnp.where(lane == c + 1, mid, jnp.where(lane == c + 2, lo, cols)))
        in_slc_block = lane == BLOCK_ROW0 + _div_pow2(ipos, SEL_BLOCK)
        in_moba_block = lane == BLOCK_ROW0 + _div_pow2(ipos, MOBA_BLOCK)
        for w, (k_ref, onehot) in enumerate(((ks_ref, in_slc_block), (kw_ref, None), (km_ref, in_moba_block))):
            extra = cols if onehot is None else jnp.where(onehot, 1.0, cols)
            kaug_ref[w] = jnp.concatenate([k_ref[0, 0, 0], extra.astype(BF16)], axis=1)
        km = km_ref[0, 0, 0].astype(F32)
        kmean_ref[...] = jnp.mean(km.reshape(n_moba, MOBA_BLOCK, HEAD_DIM), axis=1)

    wide = N_REP * TQ
    per_rep = lambda f: jnp.concatenate([f(r) for r in range(N_REP)], axis=1)
    q_in_tile = lax.broadcasted_iota(jnp.int32, (1, wide), 1) & (TQ - 1)
    tq = t0 + q_in_tile
    causal = lax.broadcasted_iota(jnp.int32, (TQ, wide), 0) <= q_in_tile
    gates = jax.nn.sigmoid(gt_ref[0])
    gate_row = lambda branch: per_rep(lambda r: gates[3 * r + branch:3 * r + branch + 1, :])
    slope_row = per_rep(lambda r: jnp.full((1, TQ), slopes_ref[g * N_REP + r], F32))
    aug_row = lax.broadcasted_iota(jnp.int32, (HEAD_DIM, wide), 0)
    aug_lo = ALIBI_PARTS * _div_pow2(lax.broadcasted_iota(jnp.int32, (HEAD_DIM, wide), 1), TQ)
    ones = jnp.where((aug_row >= aug_lo) & (aug_row < aug_lo + ALIBI_PARTS), 1.0, 0.0)[:BLOCK_ROW0]
    qn = per_rep(lambda r: qn_ref[0, r * HEAD_DIM:(r + 1) * HEAD_DIM, :])
    qm = per_rep(lambda r: qm_ref[0, r * HEAD_DIM:(r + 1) * HEAD_DIM, :])
    init = (jnp.full((1, wide), NEG_BIG, F32), jnp.zeros((1, wide), F32), jnp.zeros((HEAD_DIM, wide), F32))

    def augment(q, block_bias):
        pad = jnp.zeros((HEAD_DIM - BLOCK_ROW0 - block_bias.shape[0], wide), F32)
        return jnp.concatenate([q, jnp.concatenate([ones, block_bias, pad], axis=0).astype(BF16)], axis=0)

    kc = kc_ref[0, 0].astype(BF16)
    vct = vct_ref[0, 0].astype(BF16)
    k_end = lax.broadcasted_iota(jnp.int32, (n_cmp, wide), 0) * CMP_BLOCK + (CMP_BLOCK - 1)
    p = _masked_softmax(_dot(kc, qn) + slope_row * k_end.astype(F32), k_end <= tq, 0)
    o_cmp = _dot(vct, p.astype(BF16))
    imp = p[:, 0:TQ]
    for r in range(1, N_REP):
        imp = imp + p[:, r * TQ:(r + 1) * TQ]

    blk = lax.broadcasted_iota(jnp.int32, (n_cmp, TQ), 0)
    tq1 = tq[:, 0:TQ]
    forced = (blk == _div_pow2(tq1, SEL_BLOCK)) | (blk == 0)
    visible = (blk * CMP_BLOCK + (CMP_BLOCK - 1)) <= tq1
    score = jnp.where(forced, jnp.inf, jnp.where(visible, imp, -jnp.inf))
    qn_aug = augment(qn, jnp.concatenate([_topk_bias(score, N_SEL, blk, 0)] * N_REP, axis=1))

    mblk = lax.broadcasted_iota(jnp.int32, (n_moba, wide), 0)
    gate = _dot(kmean_ref[...].astype(BF16), qm)
    moba_bias = _topk_bias(jnp.where(mblk < qt, gate, -jnp.inf), MOBA_TOPK, mblk, 0)
    qm_aug = augment(qm, jnp.where(mblk == qt, 0.0, moba_bias))

    def tile_body(kt, carry):
        start = pl.multiple_of(kt * TQ, TQ)
        slc = _flash_t(carry[0], kaug_ref[0, pl.ds(start, TQ), :], qn_aug, vs_ref[0, :, pl.ds(start, TQ)], None)
        moba = _flash_t(carry[1], kaug_ref[2, pl.ds(start, TQ), :], qm_aug, vm_ref[0, :, pl.ds(start, TQ)], None)
        return slc, moba

    slc, moba = lax.fori_loop(0, qt, tile_body, (init, init))
    d_start = pl.multiple_of(t0, TQ)
    _, l, acc = _flash_t(slc, kaug_ref[0, pl.ds(d_start, TQ), :], qn_aug, vs_ref[0, :, pl.ds(d_start, TQ)], causal)
    o_slc = acc / l
    _, l, acc = _flash_t(moba, kaug_ref[2, pl.ds(d_start, TQ), :], qm_aug, vm_ref[0, :, pl.ds(d_start, TQ)], causal)
    o_moba = acc / l

    win = init
    for back in range(WINDOW // TQ, 0, -1):
        kt = qt - back
        start = pl.multiple_of(jnp.maximum(kt, 0) * TQ, TQ)
        mask = jnp.logical_not(causal) if back == WINDOW // TQ else None
        win = _flash_t(win, kaug_ref[1, pl.ds(start, TQ), :], qn_aug, vw_ref[0, :, pl.ds(start, TQ)], mask,
                       jnp.where(kt >= 0, 0.0, NEG_BIG))
    _, l, acc = _flash_t(win, kaug_ref[1, pl.ds(d_start, TQ), :], qn_aug, vw_ref[0, :, pl.ds(d_start, TQ)], causal)
    o_win = acc / l

    o_nsa = gate_row(0) * o_cmp + gate_row(1) * o_slc + gate_row(2) * o_win
    for r in range(N_REP):
        on_ref[0, r * HEAD_DIM:(r + 1) * HEAD_DIM, :] = o_nsa[:, r * TQ:(r + 1) * TQ]
        om_ref[0, r * HEAD_DIM:(r + 1) * HEAD_DIM, :] = o_moba[:, r * TQ:(r + 1) * TQ]


def attn_prompt(slopes, q_fm, k_tm, kv_fm, k_cmp, v_cmp_t, gt_fm):
    b, _, seq = q_fm.shape
    assert MOBA_BLOCK == TQ and seq % TQ == 0 and seq >= WINDOW + TQ
    n_cmp = seq // CMP_BLOCK
    assert ALIBI_PARTS * N_REP <= BLOCK_ROW0 and BLOCK_ROW0 + n_cmp <= HEAD_DIM
    q_spec = lambda off: pl.BlockSpec((1, KV_W, TQ), lambda bi, g, qt: (bi, off + g, qt))
    k_spec = lambda w: pl.BlockSpec((1, 1, 1, seq, HEAD_DIM), lambda bi, g, qt: (bi, w, g, 0, 0))
    v_spec = lambda state: pl.BlockSpec((1, HEAD_DIM, seq), lambda bi, g, qt: (bi, state * N_KV + g, 0))
    o_spec = pl.BlockSpec((1, KV_W, TQ), lambda bi, g, qt: (bi, g, qt))
    o_shape = jax.ShapeDtypeStruct((b, Q_W, seq), F32)
    return pl.pallas_call(
        functools.partial(_attn_prompt_kernel, seq=seq),
        grid=(b, N_KV, seq // TQ),
        in_specs=[pl.BlockSpec(memory_space=pltpu.SMEM), q_spec(0), q_spec(N_KV),
                  k_spec(0), k_spec(1), k_spec(2), v_spec(3), v_spec(7), v_spec(5),
                  pl.BlockSpec((1, 1, n_cmp, HEAD_DIM), lambda bi, g, qt: (bi, g, 0, 0)),
                  pl.BlockSpec((1, 1, HEAD_DIM, n_cmp), lambda bi, g, qt: (bi, g, 0, 0)),
                  pl.BlockSpec((1, GATE_PAD, TQ), lambda bi, g, qt: (bi, g, qt))],
        out_specs=[o_spec, o_spec],
        out_shape=[o_shape, o_shape],
        scratch_shapes=[pltpu.VMEM((3, seq, 2 * HEAD_DIM), BF16), pltpu.VMEM((seq // MOBA_BLOCK, HEAD_DIM), F32)],
        compiler_params=_cparams("parallel", "parallel", "arbitrary"),
        name="attn_prompt",
    )(slopes, q_fm, q_fm, k_tm, k_tm, k_tm, kv_fm, kv_fm, kv_fm, k_cmp, v_cmp_t, gt_fm)


def _flash_update(carry, q, k, v, bias):
    m, l, acc = carry
    s = _dot_t(q, k) + bias
    m_new = jnp.maximum(m, jnp.max(s, axis=-1, keepdims=True))
    alpha = jnp.exp(m - m_new)
    p = jnp.exp(s - m_new)
    l = alpha * l + jnp.sum(p, axis=-1, keepdims=True)
    return m_new, l, alpha * acc + _dot(p.astype(BF16), v)


def _diag_blocks(o, rows_per_group):
    return jnp.concatenate(
        [o[g * rows_per_group:(g + 1) * rows_per_group, g * HEAD_DIM:(g + 1) * HEAD_DIM] for g in range(N_KV)], axis=0)


def _page_spec(layer, n_pool, n_pages, pages_per_step, i):
    return pl.BlockSpec((1, KV_W, PAGE_SIZE),
                        lambda bi, s, pt: (layer * n_pool + pt[bi * n_pages + s * pages_per_step + i], 0, 0))


def _block_sum_kernel(pt_ref, *refs, pages_per_step):
    del pt_ref
    o_ref = refs[pages_per_step]
    step = pl.program_id(1)

    @pl.when(step == 0)
    def _():
        o_ref[...] = jnp.zeros(o_ref.shape, F32)

    lane = lax.broadcasted_iota(jnp.int32, (KV_W, o_ref.shape[2]), 1)
    acc = o_ref[0]
    for i in range(pages_per_step):
        blk = _div_pow2(step * pages_per_step + i, MOBA_BLOCK // PAGE_SIZE)
        acc = acc + jnp.where(lane == blk, jnp.sum(refs[i][0], axis=1, keepdims=True), 0.0)
    o_ref[0] = acc


def moba_block_sums(pool, layer, page_table, pages_per_step, n_lanes):
    b, n_pages = page_table.shape
    n_pool = pool.shape[0] // DEPTH
    assert n_pages % pages_per_step == 0 and n_pages * PAGE_SIZE // MOBA_BLOCK <= n_lanes
    return pl.pallas_call(
        functools.partial(_block_sum_kernel, pages_per_step=pages_per_step),
        grid_spec=pltpu.PrefetchScalarGridSpec(
            num_scalar_prefetch=1,
            grid=(b, n_pages // pages_per_step),
            in_specs=[_page_spec(layer, n_pool, n_pages, pages_per_step, i) for i in range(pages_per_step)],
            out_specs=pl.BlockSpec((1, KV_W, n_lanes), lambda bi, s, pt: (bi, 0, 0)),
        ),
        out_shape=jax.ShapeDtypeStruct((b, KV_W, n_lanes), F32),
        compiler_params=_cparams("parallel", "arbitrary"),
        name="moba_block_sums",
    )(page_table.reshape(-1), *([pool] * pages_per_step))


def _cmp_pages_kernel(pt_ref, *refs, n_pages):
    del pt_ref
    page_refs = refs[:n_pages]
    pos_ref, w1_ref, w2_ref, o_ref, buf_ref = refs[n_pages:]
    blocks_per_page = PAGE_SIZE // CMP_BLOCK
    hid = w2_ref.shape[0]
    for i in range(n_pages):
        buf_ref[i * KV_W:(i + 1) * KV_W, :] = page_refs[i][0]
    w2 = w2_ref[...]
    for g in range(N_KV):

        def body(j, acc, g=g):
            pieces = []
            for t in range(CMP_DIMS_PER_DOT):
                d = j * CMP_DIMS_PER_DOT + t
                x = buf_ref[pl.ds(g * HEAD_DIM + d, n_pages, stride=KV_W), :]
                x = (x + pos_ref[pl.ds(d, 1), :]).astype(BF16)
                pieces.append(jnp.concatenate([x[:, i * CMP_BLOCK:(i + 1) * CMP_BLOCK] for i in range(blocks_per_page)], axis=0))
            return acc + _dot(jnp.concatenate(pieces, axis=1), w1_ref[j])

        h = lax.fori_loop(0, HEAD_DIM // CMP_DIMS_PER_DOT, body, jnp.zeros((blocks_per_page * n_pages, hid), F32), unroll=4)
        h = (h * jax.nn.sigmoid(h)).astype(BF16)
        y = _dot(h, w2)
        for blk in range(blocks_per_page):
            o_ref[0, blk, :, g * HEAD_DIM:(g + 1) * HEAD_DIM] = y[blk * n_pages:(blk + 1) * n_pages, :]


def cmp_pages(pool, layer, page_table, pos_t, w1_by_dim, w2):
    b, n_pages = page_table.shape
    n_pool = pool.shape[0] // DEPTH
    blocks_per_page = PAGE_SIZE // CMP_BLOCK
    page_spec = lambda i: pl.BlockSpec((1, KV_W, PAGE_SIZE), lambda bi, pt: (layer * n_pool + pt[bi * n_pages + i], 0, 0))
    full = lambda a: pl.BlockSpec(a.shape, lambda bi, pt: (0,) * a.ndim)
    return pl.pallas_call(
        functools.partial(_cmp_pages_kernel, n_pages=n_pages),
        grid_spec=pltpu.PrefetchScalarGridSpec(
            num_scalar_prefetch=1,
            grid=(b,),
            in_specs=[page_spec(i) for i in range(n_pages)] + [full(pos_t), full(w1_by_dim), full(w2)],
            out_specs=pl.BlockSpec((1, blocks_per_page, n_pages, KV_W), lambda bi, pt: (bi, 0, 0, 0)),
            scratch_shapes=[pltpu.VMEM((n_pages * KV_W, PAGE_SIZE), F32)],
        ),
        out_shape=jax.ShapeDtypeStruct((b, blocks_per_page, n_pages, KV_W), F32),
        compiler_params=_cparams("parallel"),
        name="cmp_pages",
    )(page_table.reshape(-1), *([pool] * n_pages), pos_t, w1_by_dim, w2)


def _cmp_w1_by_dim(w1):
    w = w1.reshape(CMP_BLOCK, HEAD_DIM, w1.shape[1]).transpose(1, 0, 2)
    return w.reshape(HEAD_DIM // CMP_DIMS_PER_DOT, CMP_DIMS_PER_DOT * CMP_BLOCK, w1.shape[1]).astype(BF16)


def _sample_select_kernel(qn_ref, qm_ref, kc_ref, vc_ref, kmean_ref, slope_ref, tpos_ref,
                          ocmp_ref, sbias_ref, mbias_ref, *, past, n_new):
    rows = N_HEADS * n_new
    rpg = N_REP * n_new
    n_cmp = kc_ref.shape[1]
    n_sel_lanes = sbias_ref.shape[2]
    n_moba_lanes = mbias_ref.shape[2]
    slope = slope_ref[...]
    qpos = past + tpos_ref[...]

    blk = lax.broadcasted_iota(jnp.int32, (rows, n_cmp), 1)
    dist = qpos - (blk * CMP_BLOCK + (CMP_BLOCK - 1))
    p = _masked_softmax(_dot_t(qn_ref[0], kc_ref[0].astype(BF16)) - slope * dist.astype(F32), dist >= 0, -1)
    ocmp_ref[0] = _diag_blocks(_dot(p.astype(BF16), vc_ref[0].astype(BF16)), rpg)

    imp = jnp.sum(p.reshape(N_KV, N_REP, n_new, n_cmp), axis=1).reshape(N_KV * n_new, n_cmp)
    imp = jnp.concatenate([imp, jnp.zeros((N_KV * n_new, n_sel_lanes - n_cmp), F32)], axis=1)
    sblk = lax.broadcasted_iota(jnp.int32, (N_KV * n_new, n_sel_lanes), 1)
    assert n_new & (n_new - 1) == 0
    spos = past + (lax.broadcasted_iota(jnp.int32, (N_KV * n_new, 1), 0) & (n_new - 1))
    n_sel_blocks = -(-(past + n_new) // SEL_BLOCK)
    visible = (sblk * SEL_BLOCK + (SEL_BLOCK - 1)) <= spos
    forced = (sblk == _div_pow2(spos, SEL_BLOCK)) | (sblk == 0)
    score = jnp.where(forced, jnp.inf, jnp.where(visible, imp, -jnp.inf))
    score = jnp.where(sblk < n_sel_blocks, score, -jnp.inf)
    sb = _topk_bias(score, N_SEL, sblk, -1)
    sb = jnp.broadcast_to(sb.reshape(N_KV, 1, n_new, n_sel_lanes), (N_KV, N_REP, n_new, n_sel_lanes))
    sbias_ref[0] = sb.reshape(rows, n_sel_lanes)

    mblk = lax.broadcasted_iota(jnp.int32, (rows, n_moba_lanes), 1)
    own = _div_pow2(qpos, MOBA_BLOCK)
    gate = _dot(qm_ref[0], kmean_ref[0].astype(BF16))
    mscore = jnp.where(mblk == own, jnp.inf, jnp.where(mblk < own, gate, -jnp.inf))
    mbias_ref[0] = _topk_bias(mscore, MOBA_TOPK + 1, mblk, -1)


def sample_select(qn, qm, kc, vc, kmean_t, slope, tpos, past, n_new, n_sel_lanes):
    b, rows, _ = qn.shape

    def spec(a):
        return pl.BlockSpec((1,) + a.shape[1:], lambda bi: (bi,) + (0,) * (a.ndim - 1))

    def full(a):
        return pl.BlockSpec(a.shape, lambda bi: (0,) * a.ndim)

    outs = [jax.ShapeDtypeStruct((b, rows, HEAD_DIM), F32),
            jax.ShapeDtypeStruct((b, rows, n_sel_lanes), F32),
            jax.ShapeDtypeStruct((b, rows, kmean_t.shape[2]), F32)]
    return pl.pallas_call(
        functools.partial(_sample_select_kernel, past=past, n_new=n_new),
        grid=(b,),
        in_specs=[spec(qn), spec(qm), spec(kc), spec(vc), spec(kmean_t), full(slope), full(tpos)],
        out_specs=[spec(o) for o in outs],
        out_shape=outs,
        compiler_params=_cparams("parallel"),
        name="sample_select",
    )(qn, qm, kc, vc, kmean_t, slope, tpos)


def _attn_sample_kernel(pt_ref, *refs, pages_per_step, past, n_new):
    del pt_ref
    pps = pages_per_step
    (qn_ref, qm_ref, sbias_ref, mbias_ref, slope_ref, tpos_ref) = refs[:6]
    page_refs = refs[6:6 + 4 * pps]
    (ksn_ref, vsn_ref, kmn_ref, vmn_ref, kw_ref, vw_ref, ocmp_ref, gt_ref) = refs[6 + 4 * pps:14 + 4 * pps]
    on_ref, om_ref = refs[14 + 4 * pps:16 + 4 * pps]
    ms_ref, ls_ref, as_ref, mm_ref, lm_ref, am_ref = refs[16 + 4 * pps:]
    step = pl.program_id(1)
    rpg = N_REP * n_new
    qn = qn_ref[0]
    qm = qm_ref[0]
    slope = slope_ref[...]
    tpos = tpos_ref[...]

    @pl.when(step == 0)
    def _():
        ms_ref[...] = jnp.full(ms_ref.shape, NEG_BIG, F32)
        mm_ref[...] = jnp.full(mm_ref.shape, NEG_BIG, F32)
        ls_ref[...] = jnp.zeros(ls_ref.shape, F32)
        lm_ref[...] = jnp.zeros(lm_ref.shape, F32)
        as_ref[...] = jnp.zeros(as_ref.shape, F32)
        am_ref[...] = jnp.zeros(am_ref.shape, F32)

    sbias = sbias_ref[0].astype(BF16)
    mbias = mbias_ref[0].astype(BF16)
    n_keys = pps * PAGE_SIZE
    first_key = step * n_keys
    key = lax.broadcasted_iota(jnp.int32, (1, n_keys), 1)
    alibi = slope * (first_key - past + key).astype(F32)
    s_blk = _div_pow2(first_key + lax.broadcasted_iota(jnp.int32, (sbias.shape[1], n_keys), 1), SEL_BLOCK)
    s_expand = (lax.broadcasted_iota(jnp.int32, (sbias.shape[1], n_keys), 0) == s_blk).astype(BF16)
    m_blk = _div_pow2(first_key + lax.broadcasted_iota(jnp.int32, (mbias.shape[1], n_keys), 1), MOBA_BLOCK)
    m_expand = (lax.broadcasted_iota(jnp.int32, (mbias.shape[1], n_keys), 0) == m_blk).astype(BF16)

    def update(carry, q, k_refs, v_refs, bias):
        m, l, acc = carry
        s = jnp.concatenate([_dot(q, k_ref[0].astype(BF16)) for k_ref in k_refs], axis=1) + bias
        m_new = jnp.maximum(m, jnp.max(s, axis=-1, keepdims=True))
        alpha = jnp.exp(m - m_new)
        p = jnp.exp(s - m_new)
        l = alpha * l + jnp.sum(p, axis=-1, keepdims=True)
        acc = alpha * acc
        for i, v_ref in enumerate(v_refs):
            acc = acc + _dot_t(p[:, i * PAGE_SIZE:(i + 1) * PAGE_SIZE].astype(BF16), v_ref[0].astype(BF16))
        return m_new, l, acc

    slc = update((ms_ref[...], ls_ref[...], as_ref[...]), qn, page_refs[:pps], page_refs[pps:2 * pps],
                 alibi + _dot(sbias, s_expand))
    moba = update((mm_ref[...], lm_ref[...], am_ref[...]), qm, page_refs[2 * pps:3 * pps], page_refs[3 * pps:],
                  alibi + _dot(mbias, m_expand))
    ms_ref[...], ls_ref[...], as_ref[...] = slc
    mm_ref[...], lm_ref[...], am_ref[...] = moba

    @pl.when(step == pl.num_programs(1) - 1)
    def _():
        new = lax.broadcasted_iota(jnp.int32, (1, n_new), 1)
        bias = jnp.where(new <= tpos, slope * new.astype(F32), NEG_BIG)
        _, l, acc = _flash_update(slc, qn, ksn_ref[0].astype(BF16), vsn_ref[0].astype(BF16), bias)
        o_slc = _diag_blocks(acc / l, rpg)
        _, l, acc = _flash_update(moba, qm, kmn_ref[0].astype(BF16), vmn_ref[0].astype(BF16), bias)
        om_ref[0] = _diag_blocks(acc / l, rpg)

        n_ctx = kw_ref.shape[2]
        wrel = lax.broadcasted_iota(jnp.int32, (1, n_ctx), 1) - (n_ctx - n_new)
        dist = tpos - wrel
        valid = (dist >= 0) & (dist < WINDOW) & (wrel + past >= 0)
        s = jnp.where(valid, _dot(qn, kw_ref[0].astype(BF16)) + slope * wrel.astype(F32), NEG_BIG)
        pw = jnp.exp(s - jnp.max(s, axis=-1, keepdims=True))
        o_win = _diag_blocks(_dot_t(pw.astype(BF16), vw_ref[0].astype(BF16)) / jnp.sum(pw, axis=-1, keepdims=True), rpg)

        gates = jax.nn.sigmoid(gt_ref[0])
        on_ref[0] = gates[:, 0:1] * ocmp_ref[0] + gates[:, 1:2] * o_slc + gates[:, 2:3] * o_win


def attn_sample(page_table, layer, qn, qm, sbias, mbias, slope, tpos, pools, new_kv, kwin_t, vwin_t, ocmp, gt,
                past, n_new, pages_per_step):
    b, rows, _ = qn.shape
    n_pages = page_table.shape[1]
    n_pool = pools[0].shape[0] // DEPTH
    pps = pages_per_step
    assert n_pages % pps == 0 and n_pages * PAGE_SIZE == past

    def spec(a):
        return pl.BlockSpec((1,) + a.shape[1:], lambda bi, s, pt: (bi,) + (0,) * (a.ndim - 1))

    def full(a):
        return pl.BlockSpec(a.shape, lambda bi, s, pt: (0,) * a.ndim)

    page_specs = [_page_spec(layer, n_pool, n_pages, pps, i) for _ in range(4) for i in range(pps)]
    page_args = [pool for pool in pools for _ in range(pps)]
    o_shape = jax.ShapeDtypeStruct((b, rows, HEAD_DIM), F32)
    return pl.pallas_call(
        functools.partial(_attn_sample_kernel, pages_per_step=pps, past=past, n_new=n_new),
        grid_spec=pltpu.PrefetchScalarGridSpec(
            num_scalar_prefetch=1,
            grid=(b, n_pages // pps),
            in_specs=[spec(qn), spec(qm), spec(sbias), spec(mbias), full(slope), full(tpos)] + page_specs
                     + [spec(a) for a in new_kv] + [spec(kwin_t), spec(vwin_t), spec(ocmp), spec(gt)],
            out_specs=[spec(o_shape), spec(o_shape)],
            scratch_shapes=[pltpu.VMEM((rows, 1), F32), pltpu.VMEM((rows, 1), F32), pltpu.VMEM((rows, KV_W), F32),
                            pltpu.VMEM((rows, 1), F32), pltpu.VMEM((rows, 1), F32), pltpu.VMEM((rows, KV_W), F32)],
        ),
        out_shape=[o_shape, o_shape],
        compiler_params=_cparams("parallel", "arbitrary"),
        name="attn_sample",
    )(page_table.reshape(-1), qn, qm, sbias, mbias, slope, tpos, *page_args, *new_kv, kwin_t, vwin_t, ocmp, gt)


def _out_proj(o_nsa, o_moba, gn_ref, gm_ref, wn_ref, wm_ref, x_ref, g_ref, b_ref, o_ref):
    hn = _rms_norm(o_nsa, gn_ref[...]).astype(BF16)
    hm = _rms_norm(o_moba, gm_ref[...]).astype(BF16)
    y = _dot(hn, wn_ref[...]) + _dot(hm, wm_ref[...])
    o_ref[...] = _layer_norm(DEEPNORM_ALPHA * x_ref[...] + y, g_ref[...], b_ref[...])


def _out_proj_kernel(on_ref, om_ref, *refs):
    _out_proj(on_ref[...], om_ref[...], *refs)


def _out_proj_fm_kernel(on_ref, om_ref, *refs):
    _out_proj(on_ref[0].T, om_ref[0].T, *refs)


def out_proj_ln(o_nsa, o_moba, gn_nsa, gn_moba, w_nsa, w_moba, x, g, b, feature_major):
    m = x.shape[0]
    assert m % ROW_TILE == 0
    row = lambda w: pl.BlockSpec((ROW_TILE, w), lambda i: (i, 0))
    full = lambda a: pl.BlockSpec(a.shape, lambda i: (0,) * a.ndim)
    if feature_major:
        per_batch = o_nsa.shape[2] // ROW_TILE
        o_spec = pl.BlockSpec((1, Q_W, ROW_TILE), lambda i: (i // per_batch, 0, i % per_batch))
    else:
        o_spec = row(Q_W)
    return pl.pallas_call(
        _out_proj_fm_kernel if feature_major else _out_proj_kernel,
        grid=(m // ROW_TILE,),
        in_specs=[o_spec, o_spec, full(gn_nsa), full(gn_moba), full(w_nsa), full(w_moba),
                  row(D_MODEL), full(g), full(b)],
        out_specs=row(D_MODEL),
        out_shape=jax.ShapeDtypeStruct((m, D_MODEL), F32),
        compiler_params=_cparams("parallel"),
        name="out_proj_ln",
    )(o_nsa, o_moba, gn_nsa, gn_moba, w_nsa, w_moba, x, g, b)


def _mem_attn_kernel(x_ref, wq_ref, k_ref, v_ref, wo_ref, g_ref, b_ref, o_ref):
    x = x_ref[0]
    q = _dot(x.astype(BF16), wq_ref[...]).astype(BF16)
    k = k_ref[0].astype(BF16)
    v = v_ref[0].astype(BF16)
    heads = []
    for h in range(MEM_HEADS):
        sl = slice(h * MEM_HEAD_DIM, (h + 1) * MEM_HEAD_DIM)
        s = _dot_t(q[:, sl], k[:, sl]) * (MEM_HEAD_DIM ** -0.5)
        e = jnp.exp(s - jnp.max(s, axis=-1, keepdims=True))
        p = e / jnp.sum(e, axis=-1, keepdims=True)
        heads.append(_dot(p.astype(BF16), v[:, sl]))
    o = jnp.concatenate(heads, axis=-1).astype(BF16)
    o_ref[0] = _layer_norm(DEEPNORM_ALPHA * x + _dot(o, wo_ref[...]), g_ref[...], b_ref[...])


def mem_attn_ln(x, wq, mem_k, mem_v, wo, g, b, tq):
    bsz, t, d = x.shape
    n_mem, d_mem = mem_k.shape[1:]
    assert t % tq == 0
    full = lambda a: pl.BlockSpec(a.shape, lambda bi, i: (0,) * a.ndim)
    return pl.pallas_call(
        _mem_attn_kernel,
        grid=(bsz, t // tq),
        in_specs=[pl.BlockSpec((1, tq, d), lambda bi, i: (bi, i, 0)), full(wq),
                  pl.BlockSpec((1, n_mem, d_mem), lambda bi, i: (bi, 0, 0)),
                  pl.BlockSpec((1, n_mem, d_mem), lambda bi, i: (bi, 0, 0)), full(wo), full(g), full(b)],
        out_specs=pl.BlockSpec((1, tq, d), lambda bi, i: (bi, i, 0)),
        out_shape=jax.ShapeDtypeStruct(x.shape, F32),
        compiler_params=_cparams("parallel", "arbitrary"),
        name="mem_attn_ln",
    )(x, wq, mem_k, mem_v, wo, g, b)


def _router_kernel(x_ref, w_ref, b_ref, e_ref, g_ref):
    logits = jnp.dot(x_ref[...], w_ref[...], preferred_element_type=F32, precision=lax.Precision.HIGHEST) + b_ref[...]
    lane = lax.broadcasted_iota(jnp.int32, logits.shape, 1).astype(F32)
    cur = logits
    vals, idxs = [], []
    for _ in range(TOP_K):
        m = jnp.max(cur, axis=-1, keepdims=True)
        idx = jnp.min(jnp.where(cur == m, lane, float(N_EXPERTS)), axis=-1, keepdims=True)
        vals.append(m)
        idxs.append(idx)
        cur = jnp.where(lane == idx, -jnp.inf, cur)
    top_v = jnp.concatenate(vals, axis=-1)
    e = jnp.exp(top_v - vals[0])
    g_ref[...] = e / jnp.sum(e, axis=-1, keepdims=True)
    e_ref[...] = jnp.concatenate(idxs, axis=-1).astype(jnp.int32)


def router(x, w, b):
    m, d = x.shape
    assert m % ROW_TILE == 0
    return pl.pallas_call(
        _router_kernel,
        grid=(m // ROW_TILE,),
        in_specs=[pl.BlockSpec((ROW_TILE, d), lambda i: (i, 0)),
                  pl.BlockSpec(w.shape, lambda i: (0, 0)), pl.BlockSpec(b.shape, lambda i: (0, 0))],
        out_specs=[pl.BlockSpec((ROW_TILE, TOP_K), lambda i: (i, 0)), pl.BlockSpec((ROW_TILE, TOP_K), lambda i: (i, 0))],
        out_shape=[jax.ShapeDtypeStruct((m, TOP_K), jnp.int32), jax.ShapeDtypeStruct((m, TOP_K), F32)],
        compiler_params=_cparams("parallel"),
        name="router",
    )(x, w, b)


def _dispatch_kernel(nu_ref, src_ref, src_next_ref, x_hbm, o_ref, buf_ref, sem_ref):
    i = pl.program_id(0)
    n_used = nu_ref[0]
    slot = i & 1

    def row_copy(row, r, slot):
        return pltpu.make_async_copy(x_hbm.at[pl.ds(row, 1), :], buf_ref.at[slot, pl.ds(r, 1), :], sem_ref.at[slot])

    def start_gather(rows_ref, slot):
        def body(r, carry):
            row_copy(rows_ref[0, 0, r], r, slot).start()
            return carry

        lax.fori_loop(0, MOE_TILE, body, 0, unroll=8)

    @pl.when((i == 0) & (n_used > 0))
    def _():
        start_gather(src_ref, 0)

    @pl.when(i + 1 < n_used)
    def _():
        start_gather(src_next_ref, 1 - slot)

    @pl.when(i < n_used)
    def _():
        def body(r, carry):
            row_copy(0, r, slot).wait()
            return carry

        lax.fori_loop(0, MOE_TILE, body, 0, unroll=8)
        o_ref[...] = buf_ref[slot].astype(BF16)

    @pl.when(i >= n_used)
    def _():
        o_ref[...] = jnp.zeros(o_ref.shape, BF16)


def dispatch_rows(x, src, n_used):
    n, d = x.shape
    n_blocks = src.shape[0] // MOE_TILE
    src3 = src.reshape(n_blocks, 1, MOE_TILE)
    return pl.pallas_call(
        _dispatch_kernel,
        grid_spec=pltpu.PrefetchScalarGridSpec(
            num_scalar_prefetch=1,
            grid=(n_blocks,),
            in_specs=[pl.BlockSpec((1, 1, MOE_TILE), lambda i, nu: (i, 0, 0), memory_space=pltpu.SMEM),
                      pl.BlockSpec((1, 1, MOE_TILE), lambda i, nu: (jnp.minimum(i + 1, n_blocks - 1), 0, 0),
                                   memory_space=pltpu.SMEM),
                      pl.BlockSpec(memory_space=pl.ANY)],
            out_specs=pl.BlockSpec((MOE_TILE, d), lambda i, nu: (i, 0)),
            scratch_shapes=[pltpu.VMEM((2, MOE_TILE, d), F32), pltpu.SemaphoreType.DMA((2,))],
        ),
        out_shape=jax.ShapeDtypeStruct((n_blocks * MOE_TILE, d), BF16),
        compiler_params=_cparams("arbitrary"),
        name="dispatch_rows",
    )(n_used, src3, src3, x)


def _expert_up_kernel(be_ref, nu_ref, x_ref, wg_ref, wu_ref, bg_ref, bu_ref, h_ref):
    del be_ref

    @pl.when(pl.program_id(1) < nu_ref[0])
    def _():
        x = x_ref[...]
        g = _dot(x, wg_ref[0].astype(BF16)) + bg_ref[0]
        u = _dot(x, wu_ref[0].astype(BF16)) + bu_ref[0]
        g = jnp.minimum(g, SWIGLU_LIMIT)
        u = jnp.clip(u, -SWIGLU_LIMIT, SWIGLU_LIMIT)
        h_ref[...] = (g * jax.nn.sigmoid(SWIGLU_ALPHA * g) * (u + 1.0)).astype(BF16)

    @pl.when(pl.program_id(1) >= nu_ref[0])
    def _():
        h_ref[...] = jnp.zeros(h_ref.shape, BF16)


def _expert_down_kernel(be_ref, nu_ref, h_ref, wd_ref, bd_ref, y_ref):
    del be_ref

    @pl.when(pl.program_id(1) < nu_ref[0])
    def _():
        y_ref[...] = _dot(h_ref[...], wd_ref[0].astype(BF16)) + bd_ref[0]

    @pl.when(pl.program_id(1) >= nu_ref[0])
    def _():
        y_ref[...] = jnp.zeros(y_ref.shape, F32)


def expert_ffn(xs, blk_e, n_used, layer, wg, bg, wu, bu, wd, bd):
    cap, d = xs.shape
    n_blocks = cap // MOE_TILE
    d_ff = wg.shape[-1]
    we = lambda j, i, be, nu: layer * N_EXPERTS + be[i]
    h = pl.pallas_call(
        _expert_up_kernel,
        grid_spec=pltpu.PrefetchScalarGridSpec(
            num_scalar_prefetch=2,
            grid=(d_ff // FF_CHUNK, n_blocks),
            in_specs=[pl.BlockSpec((MOE_TILE, d), lambda j, i, be, nu: (i, 0)),
                      pl.BlockSpec((1, d, FF_CHUNK), lambda j, i, be, nu: (we(j, i, be, nu), 0, j)),
                      pl.BlockSpec((1, d, FF_CHUNK), lambda j, i, be, nu: (we(j, i, be, nu), 0, j)),
                      pl.BlockSpec((1, 1, FF_CHUNK), lambda j, i, be, nu: (we(j, i, be, nu), 0, j)),
                      pl.BlockSpec((1, 1, FF_CHUNK), lambda j, i, be, nu: (we(j, i, be, nu), 0, j))],
            out_specs=pl.BlockSpec((MOE_TILE, FF_CHUNK), lambda j, i, be, nu: (i, j)),
        ),
        out_shape=jax.ShapeDtypeStruct((cap, d_ff), BF16),
        compiler_params=_cparams("parallel", "arbitrary"),
        name="expert_up",
    )(blk_e, n_used, xs, wg, wu, bg, bu)
    return pl.pallas_call(
        _expert_down_kernel,
        grid_spec=pltpu.PrefetchScalarGridSpec(
            num_scalar_prefetch=2,
            grid=(d // DOWN_CHUNK, n_blocks),
            in_specs=[pl.BlockSpec((MOE_TILE, d_ff), lambda j, i, be, nu: (i, 0)),
                      pl.BlockSpec((1, d_ff, DOWN_CHUNK), lambda j, i, be, nu: (we(j, i, be, nu), 0, j)),
                      pl.BlockSpec((1, 1, DOWN_CHUNK), lambda j, i, be, nu: (we(j, i, be, nu), 0, j))],
            out_specs=pl.BlockSpec((MOE_TILE, DOWN_CHUNK), lambda j, i, be, nu: (i, j)),
        ),
        out_shape=jax.ShapeDtypeStruct((cap, d), F32),
        compiler_params=_cparams("parallel", "arbitrary"),
        name="expert_down",
    )(blk_e, n_used, h, wd, bd)


def _residual_ln_kernel(x_ref, m_ref, g_ref, b_ref, o_ref):
    o_ref[...] = _layer_norm(DEEPNORM_ALPHA * x_ref[...] + m_ref[...], g_ref[...], b_ref[...])


def residual_ln(x, m, g, b):
    n, d = x.shape
    row = pl.BlockSpec((ROW_TILE, d), lambda i: (i, 0))
    full = lambda a: pl.BlockSpec(a.shape, lambda i: (0,) * a.ndim)
    return pl.pallas_call(
        _residual_ln_kernel,
        grid=(n // ROW_TILE,),
        in_specs=[row, row, full(g), full(b)],
        out_specs=row,
        out_shape=jax.ShapeDtypeStruct((n, d), F32),
        compiler_params=_cparams("parallel"),
        name="residual_ln",
    )(x, m, g, b)


def moe_ffn(x, layer, router_w, router_b, wg, bg, wu, bu, wd, bd):
    n, d = x.shape
    top_e, gate = router(x, router_w, router_b)
    a = n * TOP_K
    e_flat = top_e.reshape(a)
    onehot = (e_flat[:, None] == jnp.arange(N_EXPERTS, dtype=jnp.int32)[None, :]).astype(jnp.int32)
    csum = jnp.cumsum(onehot, axis=0)
    rank = jnp.sum((csum - onehot) * onehot, axis=1)
    counts = csum[-1]
    padded = (counts + MOE_TILE - 1) // MOE_TILE * MOE_TILE
    pad_end = jnp.cumsum(padded)
    pad_start = pad_end - padded
    dest = pad_start[e_flat] + rank
    n_blocks = -(-a // MOE_TILE) + N_EXPERTS
    cap = n_blocks * MOE_TILE
    tok = jnp.arange(a, dtype=jnp.int32) // TOP_K
    src = jnp.zeros((cap,), jnp.int32).at[dest].set(tok)
    blk_start = jnp.arange(n_blocks, dtype=jnp.int32) * MOE_TILE
    blk_e = jnp.minimum(jnp.sum(blk_start[:, None] >= pad_end[None, :], axis=1), N_EXPERTS - 1).astype(jnp.int32)
    n_used = (pad_end[-1] // MOE_TILE).astype(jnp.int32).reshape(1)
    xs = dispatch_rows(x, src, n_used)
    y = expert_ffn(xs, blk_e, n_used, layer, wg, bg, wu, bu, wd, bd)
    return jnp.sum(y[dest.reshape(n, TOP_K).T] * gate.T[:, :, None], axis=0)


def _alibi_slopes():
    return jnp.asarray(2.0 ** (-8.0 * np.arange(1, N_HEADS + 1) / N_HEADS), F32)


def _split_w_in(w):
    q_n, kv_n, gt, q_m, kv_m = jnp.split(w, [Q_W, Q_W + 6 * KV_W, Q_W + 6 * KV_W + 3 * N_HEADS,
                                            2 * Q_W + 6 * KV_W + 3 * N_HEADS], axis=1)
    kc, vc, ks, vs, kw, vw = jnp.split(kv_n, 6, axis=1)
    km, vm = jnp.split(kv_m, 2, axis=1)
    return q_n, (kc, vc, ks, vs, kw, vw, km, vm), gt, q_m


def _block_diag_rows(q, b, t):
    q = (q * (HEAD_DIM ** -0.5)).astype(BF16).reshape(b, t, N_KV, N_REP, HEAD_DIM).transpose(0, 2, 3, 1, 4)
    eye = jnp.eye(N_KV, dtype=BF16)
    bd = q[:, :, :, :, None, :] * eye[None, :, None, None, :, None]
    return bd.reshape(b, N_HEADS * t, KV_W)


def _rows_to_tokens(o, b, t):
    return o.reshape(b, N_KV, N_REP, t, HEAD_DIM).transpose(0, 3, 1, 2, 4).reshape(b * t, Q_W)


def _token_minor(a):
    lead = a.shape[:-2]
    a = a.reshape(lead + (N_KV, HEAD_DIM, a.shape[-1]))
    n = len(lead)
    return a.transpose(tuple(range(n)) + (n + 2, n, n + 1))


def kernel(x_prompt, x_sample, cache_nsa_cmp_k, cache_nsa_cmp_v, cache_nsa_slc_k, cache_nsa_slc_v, cache_moba_k, cache_moba_v, cache_nsa_win_k, cache_nsa_win_v, cache_mem_k, cache_mem_v, page_table, mem_prompt, w_in, nsa_cmp_pos, nsa_cmp_k_w1, nsa_cmp_k_w2, nsa_cmp_v_w1, nsa_cmp_v_w2, gn_nsa, gn_moba, w_out, ln1_g, ln1_b, ca_wq, ca_wk, ca_wv, ca_wo, ln2_g, ln2_b, router_w, router_b, exp_wg, exp_bg, exp_wu, exp_bu, exp_wd, exp_bd, ln3_g, ln3_b):
    bp, tp, d = x_prompt.shape
    bs, ts, _ = x_sample.shape
    n_p, n_s = bp * tp, bs * ts
    n_mem = mem_prompt.shape[1]
    depth, n_pool = cache_nsa_cmp_k.shape[:2]
    n_pages = page_table.shape[1]
    past = n_pages * PAGE_SIZE
    wbuf = cache_nsa_win_k.shape[2]
    d_mem = MEM_HEADS * MEM_HEAD_DIM
    slopes = _alibi_slopes()
    fm = lambda c: c.transpose(0, 1, 3, 4, 2).reshape(c.shape[0], c.shape[1], KV_W, c.shape[2])
    pool = lambda c: fm(c).reshape(depth * n_pool, KV_W, PAGE_SIZE)
    pools = tuple(map(pool, (cache_nsa_slc_k, cache_nsa_slc_v, cache_moba_k, cache_moba_v)))
    cmp_pools = tuple(map(pool, (cache_nsa_cmp_k, cache_nsa_cmp_v)))
    win_k_fm, win_v_fm = fm(cache_nsa_win_k), fm(cache_nsa_win_v)
    wg = exp_wg.reshape((depth * N_EXPERTS,) + exp_wg.shape[2:])
    wu = exp_wu.reshape((depth * N_EXPERTS,) + exp_wu.shape[2:])
    wd = exp_wd.reshape((depth * N_EXPERTS,) + exp_wd.shape[2:])
    bg = exp_bg.reshape(depth * N_EXPERTS, 1, -1)
    bu = exp_bu.reshape(depth * N_EXPERTS, 1, -1)
    bd = exp_bd.reshape(depth * N_EXPERTS, 1, -1)

    row = np.arange(N_HEADS * ts)
    s_slope = slopes[row // ts].reshape(-1, 1)
    s_tpos = jnp.asarray((row % ts).reshape(-1, 1), jnp.int32)
    n_cmp_s = (past + ts) // CMP_BLOCK
    n_sel_lanes = -(-(-(-(past + ts) // SEL_BLOCK)) // 128) * 128
    n_moba_past = past // MOBA_BLOCK

    x_p = x_prompt
    x_s = x_sample.reshape(n_s, d)
    p_states, s_states, mem_states = [], [], []
    s_win = []
    for l in range(depth):
        q_n, kv, gt, q_m = _split_w_in(w_in[l])
        kc, vc, ks, vs, kw, vw, km, vm = kv
        pos = nsa_cmp_pos[l].reshape(1, CMP_BLOCK * HEAD_DIM)
        ck_w1, ck_w2 = nsa_cmp_k_w1[l].astype(BF16), nsa_cmp_k_w2[l].astype(BF16)
        cv_w1, cv_w2 = nsa_cmp_v_w1[l].astype(BF16), nsa_cmp_v_w2[l].astype(BF16)

        wq_t = jnp.concatenate([q_n, q_m], axis=1).T.astype(BF16)
        wkv_t = jnp.concatenate([kc, vc, ks, vs, km, vm, kw, vw], axis=1).T.astype(BF16)
        wgt_t = jnp.pad(gt.reshape(d, N_KV, 3 * N_REP), ((0, 0), (0, 0), (0, GATE_PAD - 3 * N_REP)))
        wgt_t = wgt_t.reshape(d, N_KV * GATE_PAD).T.astype(BF16)
        wk_tm = jnp.concatenate([kc, vc, ks, kw, km], axis=1).astype(BF16)
        (q_fm,) = proj_feature_major(x_p, wq_t, (BF16,), HEAD_DIM ** -0.5)
        st_fm, kv_fm = proj_feature_major(x_p, wkv_t, (F32, BF16))
        (gt_fm,) = proj_feature_major(x_p, wgt_t, (F32,))
        kc_tm, vc_tm, k_tm = proj_keys_token_major(x_p, wk_tm)
        n_cmp_p = tp // CMP_BLOCK
        k_cmp = cmp_mlp(kc_tm.reshape(bp * N_KV * n_cmp_p, CMP_BLOCK * HEAD_DIM), pos, ck_w1, ck_w2, 256)
        v_cmp = cmp_mlp(vc_tm.reshape(bp * N_KV * n_cmp_p, CMP_BLOCK * HEAD_DIM), pos, cv_w1, cv_w2, 256)
        k_cmp = k_cmp.reshape(bp, N_KV, n_cmp_p, HEAD_DIM)
        v_cmp_t = v_cmp.reshape(bp, N_KV, n_cmp_p, HEAD_DIM).transpose(0, 1, 3, 2)
        o_nsa_p, o_moba_p = attn_prompt(slopes, q_fm, k_tm, kv_fm, k_cmp, v_cmp_t, gt_fm)

        w_main = jnp.concatenate([q_n, kc, vc, ks, vs, kw, vw, q_m, km, vm], axis=1).astype(BF16)
        w_gate = jnp.pad(gt, ((0, 0), (0, 128 - 3 * N_HEADS))).astype(BF16)
        proj_s = matmul(x_s, w_main, 256, 512)
        gt_s = matmul(x_s, w_gate, 256, 128)[:, :3 * N_HEADS]
        qs_n, kc_s, vc_s, ks_s, vs_s, kw_s, vw_s, qs_m, km_s, vm_s = jnp.split(
            proj_s, [Q_W + KV_W * i for i in range(7)] + [2 * Q_W + 6 * KV_W, 2 * Q_W + 7 * KV_W], axis=1)
        sn = lambda a: a.reshape(bs, ts, KV_W)

        pos_t = jnp.tile(nsa_cmp_pos[l].T, (1, PAGE_SIZE // CMP_BLOCK))
        blocks = lambda a: a.transpose(0, 2, 1, 3).reshape(bs, n_cmp_s, KV_W)
        k_cmp_s = blocks(cmp_pages(cmp_pools[0], l, page_table, pos_t, _cmp_w1_by_dim(nsa_cmp_k_w1[l]), ck_w2))
        v_cmp_s = blocks(cmp_pages(cmp_pools[1], l, page_table, pos_t, _cmp_w1_by_dim(nsa_cmp_v_w1[l]), cv_w2))
        k_mean_t = moba_block_sums(pools[2], l, page_table, 8, 128)
        k_mean_t = k_mean_t.at[:, :, n_moba_past].set(sn(km_s).sum(axis=1)) * (1.0 / MOBA_BLOCK)
        qn_s = _block_diag_rows(qs_n, bs, ts)
        qm_s = _block_diag_rows(qs_m, bs, ts)
        o_cmp_s, sbias, mbias = sample_select(qn_s, qm_s, k_cmp_s, v_cmp_s, k_mean_t, s_slope, s_tpos, past, ts, n_sel_lanes)
        kwin_t = jnp.concatenate([win_k_fm[l], sn(kw_s).transpose(0, 2, 1)], axis=2)
        vwin_t = jnp.concatenate([win_v_fm[l], sn(vw_s).transpose(0, 2, 1)], axis=2)
        gt_rows = gt_s.reshape(bs, ts, N_KV, N_REP, 3).transpose(0, 2, 3, 1, 4).reshape(bs, N_HEADS * ts, 3)
        o_nsa_s, o_moba_s = attn_sample(
            page_table, l, qn_s, qm_s, sbias, mbias, s_slope, s_tpos, pools, (sn(ks_s), sn(vs_s), sn(km_s), sn(vm_s)),
            kwin_t, vwin_t, o_cmp_s, gt_rows, past, ts, min(SAMPLE_PAGES_PER_STEP, n_pages))

        w_o = w_out[l].astype(BF16)
        op_args = (gn_nsa[l].reshape(1, -1), gn_moba[l].reshape(1, -1), w_o[:Q_W], w_o[Q_W:])
        ln1 = (ln1_g[l].reshape(1, -1), ln1_b[l].reshape(1, -1))
        x_p = out_proj_ln(o_nsa_p, o_moba_p, *op_args, x_p.reshape(n_p, d), *ln1, True)
        x_s = out_proj_ln(_rows_to_tokens(o_nsa_s, bs, ts), _rows_to_tokens(o_moba_s, bs, ts), *op_args, x_s, *ln1, False)

        mem_kv = matmul(mem_prompt.reshape(bp * n_mem, d), jnp.concatenate([ca_wk[l], ca_wv[l]], axis=1).astype(BF16), 512, 512)
        mem_k, mem_v = mem_kv[:, :d_mem].reshape(bp, n_mem, d_mem), mem_kv[:, d_mem:].reshape(bp, n_mem, d_mem)
        wq, wo = ca_wq[l].astype(BF16), ca_wo[l].astype(BF16)
        g2, b2 = ln2_g[l].reshape(1, -1), ln2_b[l].reshape(1, -1)
        x_p = mem_attn_ln(x_p.reshape(bp, tp, d), wq, mem_k, mem_v, wo, g2, b2, ROW_TILE)
        x_s = mem_attn_ln(x_s.reshape(bs, ts, d), wq, cache_mem_k[l].reshape(bs, -1, d_mem),
                          cache_mem_v[l].reshape(bs, -1, d_mem), wo, g2, b2, ts)

        x = jnp.concatenate([x_p.reshape(n_p, d), x_s.reshape(n_s, d)], axis=0)
        m = moe_ffn(x, l, router_w[l], router_b[l].reshape(1, -1), wg, bg, wu, bu, wd, bd)
        x = residual_ln(x, m, ln3_g[l].reshape(1, -1), ln3_b[l].reshape(1, -1))
        x_p, x_s = x[:n_p].reshape(bp, tp, d), x[n_p:]

        p_states.append(st_fm)
        mem_states.append((mem_k.reshape(bp, n_mem, MEM_HEADS, MEM_HEAD_DIM), mem_v.reshape(bp, n_mem, MEM_HEADS, MEM_HEAD_DIM)))
        s_states.append([a.reshape(bs, ts, N_KV, HEAD_DIM) for a in (kc_s, vc_s, ks_s, vs_s, km_s, vm_s)])
        keep_s = min(WINDOW, wbuf + ts)
        s_win.append((kwin_t[:, :, -keep_s:], vwin_t[:, :, -keep_s:]))

    keep = min(WINDOW, tp)
    st = jnp.stack(p_states)
    p_out = [_token_minor(st[:, :, i * KV_W:(i + 1) * KV_W, :]) for i in range(6)]
    p_out += [_token_minor(st[:, :, i * KV_W:(i + 1) * KV_W, tp - keep:]) for i in (6, 7)]
    p_out += [jnp.stack([ms[i] for ms in mem_states]) for i in range(2)]
    s_out = [jnp.stack([ss[i] for ss in s_states]) for i in range(6)]
    s_out += [_token_minor(jnp.stack([w[i] for w in s_win])) for i in range(2)]
    return (x_p, x_s.reshape(bs, ts, d)) + tuple(p_out) + tuple(s_out)
```

```python
import functools

import numpy as np
import jax
import jax.numpy as jnp
from jax import lax
from jax.experimental import pallas as pl
from jax.experimental.pallas import tpu as pltpu

F32 = jnp.float32
BF16 = jnp.bfloat16

D_MODEL = 2048
HEAD_DIM = 64
N_KV = 4
N_REP = 4
N_HEADS = N_KV * N_REP
CMP_BLOCK = 64
SEL_BLOCK = 64
N_SEL = 8
WINDOW = 512
MOBA_BLOCK = 256
MOBA_TOPK = 3
MEM_HEADS = 4
MEM_HEAD_DIM = 128
N_EXPERTS = 32
TOP_K = 4
SWIGLU_LIMIT = 7.0
SWIGLU_ALPHA = 1.702
LN_EPS = 1e-5
NEG_BIG = -1e30
DEPTH = 2
DEEPNORM_ALPHA = (2 * DEPTH) ** 0.25
PAGE_SIZE = 128

KV_W = N_KV * HEAD_DIM
Q_W = N_HEADS * HEAD_DIM
N_STATE = 8
GATE_PAD = 16
ALIBI_PARTS = 3
BLOCK_ROW0 = 16
TQ = 256
VMEM_LIMIT = 56 * 1024 * 1024
ROW_TILE = 256
PROJ_TILE = 512
PROJ_TOKENS = 1024
SAMPLE_PAGES_PER_STEP = 16
CMP_DIMS_PER_DOT = 4
MOE_TILE = 512
FF_CHUNK = 1024
DOWN_CHUNK = 2048


def _cparams(*sem):
    return pltpu.CompilerParams(dimension_semantics=sem, vmem_limit_bytes=VMEM_LIMIT)


def _dot(a, b):
    return jnp.dot(a, b, preferred_element_type=F32)


def _dot_t(a, b):
    return lax.dot_general(a, b, (((1,), (1,)), ((), ())), preferred_element_type=F32)


def _div_pow2(x, n):
    assert n & (n - 1) == 0
    return lax.shift_right_logical(x, n.bit_length() - 1)


def _layer_norm(z, g, b):
    mu = jnp.mean(z, axis=-1, keepdims=True)
    zc = z - mu
    var = jnp.mean(zc * zc, axis=-1, keepdims=True)
    return zc * lax.rsqrt(var + LN_EPS) * g + b


def _rms_norm(x, g):
    return x * lax.rsqrt(jnp.mean(x * x, axis=-1, keepdims=True) + LN_EPS) * g


def _masked_softmax(s, mask, axis):
    s = jnp.where(mask, s, NEG_BIG)
    m = jnp.max(s, axis=axis, keepdims=True)
    e = jnp.where(mask, jnp.exp(s - m), 0.0)
    den = jnp.sum(e, axis=axis, keepdims=True)
    return e / jnp.where(den > 0.0, den, 1.0)


def _topk_bias(score, k, index, axis):
    index = index.astype(F32)
    cur = score
    bias = jnp.full(score.shape, NEG_BIG, F32)
    for _ in range(k):
        m = jnp.max(cur, axis=axis, keepdims=True)
        first = jnp.min(jnp.where(cur == m, index, float(score.shape[axis])), axis=axis, keepdims=True)
        pick = index == first
        bias = jnp.where(pick & (m > -jnp.inf), 0.0, bias)
        cur = jnp.where(pick, -jnp.inf, cur)
    return bias


def _mm_kernel(x_ref, w_ref, o_ref):
    o_ref[...] = _dot(x_ref[...].astype(BF16), w_ref[...])


def matmul(x, w, tm, tn):
    m, k = x.shape
    n = w.shape[1]
    tm = max(t for t in range(8, min(tm, m) + 1, 8) if m % t == 0)
    assert n % tn == 0
    return pl.pallas_call(
        _mm_kernel,
        grid=(m // tm, n // tn),
        in_specs=[pl.BlockSpec((tm, k), lambda i, j: (i, 0)),
                  pl.BlockSpec((k, tn), lambda i, j: (0, j))],
        out_specs=pl.BlockSpec((tm, tn), lambda i, j: (i, j)),
        out_shape=jax.ShapeDtypeStruct((m, n), F32),
        compiler_params=_cparams("parallel", "arbitrary"),
        name="matmul",
    )(x, w)


def _proj_fm_kernel(x_ref, w_ref, *o_refs, scale):
    r = _dot_t(w_ref[...], x_ref[0].astype(BF16))
    if scale != 1.0:
        r = r * scale
    for o_ref in o_refs:
        o_ref[0] = r.astype(o_ref.dtype)


def proj_feature_major(x, w_t, dtypes, scale=1.0):
    b, t, k = x.shape
    f = w_t.shape[0]
    tf = min(PROJ_TILE, f)
    tt = PROJ_TOKENS
    assert t % tt == 0 and f % tf == 0
    o_spec = pl.BlockSpec((1, tf, tt), lambda bi, j, fi: (bi, fi, j))
    return pl.pallas_call(
        functools.partial(_proj_fm_kernel, scale=scale),
        grid=(b, t // tt, f // tf),
        in_specs=[pl.BlockSpec((1, tt, k), lambda bi, j, fi: (bi, j, 0)),
                  pl.BlockSpec((tf, k), lambda bi, j, fi: (fi, 0))],
        out_specs=[o_spec for _ in dtypes],
        out_shape=[jax.ShapeDtypeStruct((b, f, t), dt) for dt in dtypes],
        compiler_params=_cparams("parallel", "parallel", "arbitrary"),
        name="proj_feature_major",
    )(x, w_t)


def _proj_keys_kernel(x_ref, w_ref, kc_ref, vc_ref, k_ref):
    r = _dot(x_ref[0].astype(BF16), w_ref[...])
    for g in range(N_KV):
        lo = g * HEAD_DIM
        kc_ref[0, g] = r[:, lo:lo + HEAD_DIM]
        vc_ref[0, g] = r[:, KV_W + lo:KV_W + lo + HEAD_DIM]
        for w in range(3):
            k_ref[0, w, g] = r[:, (2 + w) * KV_W + lo:(2 + w) * KV_W + lo + HEAD_DIM].astype(BF16)


def proj_keys_token_major(x, w):
    b, t, k = x.shape
    assert t % PROJ_TILE == 0 and w.shape[1] == 5 * KV_W
    c_spec = pl.BlockSpec((1, N_KV, PROJ_TILE, HEAD_DIM), lambda bi, j: (bi, 0, j, 0))
    c_shape = jax.ShapeDtypeStruct((b, N_KV, t, HEAD_DIM), F32)
    return pl.pallas_call(
        _proj_keys_kernel,
        grid=(b, t // PROJ_TILE),
        in_specs=[pl.BlockSpec((1, PROJ_TILE, k), lambda bi, j: (bi, j, 0)),
                  pl.BlockSpec(w.shape, lambda bi, j: (0, 0))],
        out_specs=[c_spec, c_spec, pl.BlockSpec((1, 3, N_KV, PROJ_TILE, HEAD_DIM), lambda bi, j: (bi, 0, 0, j, 0))],
        out_shape=[c_shape, c_shape, jax.ShapeDtypeStruct((b, 3, N_KV, t, HEAD_DIM), BF16)],
        compiler_params=_cparams("parallel", "arbitrary"),
        name="proj_keys_token_major",
    )(x, w)


def _cmp_mlp_kernel(x_ref, pos_ref, w1_ref, w2_ref, o_ref):
    x = (x_ref[...] + pos_ref[...]).astype(BF16)
    h = _dot(x, w1_ref[...])
    h = h * jax.nn.sigmoid(h)
    o_ref[...] = _dot(h.astype(BF16), w2_ref[...])


def cmp_mlp(x, pos, w1, w2, tr):
    r, k = x.shape
    hid = w1.shape[1]
    tr = min(tr, r)
    assert r % tr == 0
    return pl.pallas_call(
        _cmp_mlp_kernel,
        grid=(r // tr,),
        in_specs=[pl.BlockSpec((tr, k), lambda i: (i, 0)),
                  pl.BlockSpec((1, k), lambda i: (0, 0)),
                  pl.BlockSpec((k, hid), lambda i: (0, 0)),
                  pl.BlockSpec((hid, HEAD_DIM), lambda i: (0, 0))],
        out_specs=pl.BlockSpec((tr, HEAD_DIM), lambda i: (i, 0)),
        out_shape=jax.ShapeDtypeStruct((r, HEAD_DIM), F32),
        compiler_params=_cparams("parallel"),
        name="cmp_mlp",
    )(x, pos, w1, w2)


def _flash_t(carry, k_tile, q_aug, vt_tile, mask, shift=None):
    m, l, acc = carry
    s = _dot(k_tile, q_aug)
    if shift is not None:
        s = s + shift
    if mask is not None:
        s = jnp.where(mask, s, NEG_BIG)
    m_new = jnp.maximum(m, jnp.max(s, axis=0, keepdims=True))
    alpha = jnp.exp(m - m_new)
    p = jnp.exp(s - m_new)
    l = alpha * l + jnp.sum(p, axis=0, keepdims=True)
    acc = alpha * acc + _dot(vt_tile, p.astype(BF16))
    return m_new, l, acc


def _attn_prompt_kernel(slopes_ref, qn_ref, qm_ref, ks_ref, kw_ref, km_ref, vs_ref, vw_ref, vm_ref,
                        kc_ref, vct_ref, gt_ref, on_ref, om_ref, kaug_ref, kmean_ref, *, seq):
    g = pl.program_id(1)
    qt = pl.program_id(2)
    n_cmp = seq // CMP_BLOCK
    n_moba = seq // MOBA_BLOCK
    t0 = qt * TQ

    @pl.when(qt == 0)
    def _():
        ipos = lax.broadcasted_iota(jnp.int32, (seq, HEAD_DIM), 0)
        pos = ipos.astype(F32)
        lane = lax.broadcasted_iota(jnp.int32, (seq, HEAD_DIM), 1)
        cols = jnp.zeros((seq, HEAD_DIM), F32)
        for r in range(N_REP):
            v = pos * slopes_ref[g * N_REP + r]
            hi = v.astype(BF16).astype(F32)
            mid = (v - hi).astype(BF16).astype(F32)
            lo = v - hi - mid
            c = ALIBI_PARTS * r
            cols = jnp.where(lane == c, hi, jnp.where(lane == c + 1, mid, jnp.where(lane == c + 2, lo, cols)))
        in_slc_block = lane == BLOCK_ROW0 + _div_pow2(ipos, SEL_BLOCK)
        in_moba_block = lane == BLOCK_ROW0 + _div_pow2(ipos, MOBA_BLOCK)
        for w, (k_ref, onehot) in enumerate(((ks_ref, in_slc_block), (kw_ref, None), (km_ref, in_moba_block))):
            extra = cols if onehot is None else jnp.where(onehot, 1.0, cols)
            kaug_ref[w] = jnp.concatenate([k_ref[0, 0, 0], extra.astype(BF16)], axis=1)
        km = km_ref[0, 0, 0].astype(F32)
        kmean_ref[...] = jnp.mean(km.reshape(n_moba, MOBA_BLOCK, HEAD_DIM), axis=1)

    wide = N_REP * TQ
    per_rep = lambda f: jnp.concatenate([f(r) for r in range(N_REP)], axis=1)
    q_in_tile = lax.broadcasted_iota(jnp.int32, (1, wide), 1) & (TQ - 1)
    tq = t0 + q_in_tile
    causal = lax.broadcasted_iota(jnp.int32, (TQ, wide), 0) <= q_in_tile
    gates = jax.nn.sigmoid(gt_ref[0])
    gate_row = lambda branch: per_rep(lambda r: gates[3 * r + branch:3 * r + branch + 1, :])
    slope_row = per_rep(lambda r: jnp.full((1, TQ), slopes_ref[g * N_REP + r], F32))
    aug_row = lax.broadcasted_iota(jnp.int32, (HEAD_DIM, wide), 0)
    aug_lo = ALIBI_PARTS * _div_pow2(lax.broadcasted_iota(jnp.int32, (HEAD_DIM, wide), 1), TQ)
    ones = jnp.where((aug_row >= aug_lo) & (aug_row < aug_lo + ALIBI_PARTS), 1.0, 0.0)[:BLOCK_ROW0]
    qn = per_rep(lambda r: qn_ref[0, r * HEAD_DIM:(r + 1) * HEAD_DIM, :])
    qm = per_rep(lambda r: qm_ref[0, r * HEAD_DIM:(r + 1) * HEAD_DIM, :])
    init = (jnp.full((1, wide), NEG_BIG, F32), jnp.zeros((1, wide), F32), jnp.zeros((HEAD_DIM, wide), F32))

    def augment(q, block_bias):
        pad = jnp.zeros((HEAD_DIM - BLOCK_ROW0 - block_bias.shape[0], wide), F32)
        return jnp.concatenate([q, jnp.concatenate([ones, block_bias, pad], axis=0).astype(BF16)], axis=0)

    kc = kc_ref[0, 0].astype(BF16)
    vct = vct_ref[0, 0].astype(BF16)
    k_end = lax.broadcasted_iota(jnp.int32, (n_cmp, wide), 0) * CMP_BLOCK + (CMP_BLOCK - 1)
    p = _masked_softmax(_dot(kc, qn) + slope_row * k_end.astype(F32), k_end <= tq, 0)
    o_cmp = _dot(vct, p.astype(BF16))
    imp = p[:, 0:TQ]
    for r in range(1, N_REP):
        imp = imp + p[:, r * TQ:(r + 1) * TQ]

    blk = lax.broadcasted_iota(jnp.int32, (n_cmp, TQ), 0)
    tq1 = tq[:, 0:TQ]
    forced = (blk == _div_pow2(tq1, SEL_BLOCK)) | (blk == 0)
    visible = (blk * CMP_BLOCK + (CMP_BLOCK - 1)) <= tq1
    score = jnp.where(forced, jnp.inf, jnp.where(visible, imp, -jnp.inf))
    qn_aug = augment(qn, jnp.concatenate([_topk_bias(score, N_SEL, blk, 0)] * N_REP, axis=1))

    mblk = lax.broadcasted_iota(jnp.int32, (n_moba, wide), 0)
    gate = _dot(kmean_ref[...].astype(BF16), qm)
    moba_bias = _topk_bias(jnp.where(mblk < qt, gate, -jnp.inf), MOBA_TOPK, mblk, 0)
    qm_aug = augment(qm, jnp.where(mblk == qt, 0.0, moba_bias))

    def tile_body(kt, carry):
        start = pl.multiple_of(kt * TQ, TQ)
        slc = _flash_t(carry[0], kaug_ref[0, pl.ds(start, TQ), :], qn_aug, vs_ref[0, :, pl.ds(start, TQ)], None)
        moba = _flash_t(carry[1], kaug_ref[2, pl.ds(start, TQ), :], qm_aug, vm_ref[0, :, pl.ds(start, TQ)], None)
        return slc, moba

    slc, moba = lax.fori_loop(0, qt, tile_body, (init, init))
    d_start = pl.multiple_of(t0, TQ)
    _, l, acc = _flash_t(slc, kaug_ref[0, pl.ds(d_start, TQ), :], qn_aug, vs_ref[0, :, pl.ds(d_start, TQ)], causal)
    o_slc = acc / l
    _, l, acc = _flash_t(moba, kaug_ref[2, pl.ds(d_start, TQ), :], qm_aug, vm_ref[0, :, pl.ds(d_start, TQ)], causal)
    o_moba = acc / l

    win = init
    for back in range(WINDOW // TQ, 0, -1):
        kt = qt - back
        start = pl.multiple_of(jnp.maximum(kt, 0) * TQ, TQ)
        mask = jnp.logical_not(causal) if back == WINDOW // TQ else None
        win = _flash_t(win, kaug_ref[1, pl.ds(start, TQ), :], qn_aug, vw_ref[0, :, pl.ds(start, TQ)], mask,
                       jnp.where(kt >= 0, 0.0, NEG_BIG))
    _, l, acc = _flash_t(win, kaug_ref[1, pl.ds(d_start, TQ), :], qn_aug, vw_ref[0, :, pl.ds(d_start, TQ)], causal)
    o_win = acc / l

    o_nsa = gate_row(0) * o_cmp + gate_row(1) * o_slc + gate_row(2) * o_win
    for r in range(N_REP):
        on_ref[0, r * HEAD_DIM:(r + 1) * HEAD_DIM, :] = o_nsa[:, r * TQ:(r + 1) * TQ]
        om_ref[0, r * HEAD_DIM:(r + 1) * HEAD_DIM, :] = o_moba[:, r * TQ:(r + 1) * TQ]


def attn_prompt(slopes, q_fm, k_tm, kv_fm, k_cmp, v_cmp_t, gt_fm):
    b, _, seq = q_fm.shape
    assert MOBA_BLOCK == TQ and seq % TQ == 0 and seq >= WINDOW + TQ
    n_cmp = seq // CMP_BLOCK
    assert ALIBI_PARTS * N_REP <= BLOCK_ROW0 and BLOCK_ROW0 + n_cmp <= HEAD_DIM
    q_spec = lambda off: pl.BlockSpec((1, KV_W, TQ), lambda bi, g, qt: (bi, off + g, qt))
    k_spec = lambda w: pl.BlockSpec((1, 1, 1, seq, HEAD_DIM), lambda bi, g, qt: (bi, w, g, 0, 0))
    v_spec = lambda state: pl.BlockSpec((1, HEAD_DIM, seq), lambda bi, g, qt: (bi, state * N_KV + g, 0))
    o_spec = pl.BlockSpec((1, KV_W, TQ), lambda bi, g, qt: (bi, g, qt))
    o_shape = jax.ShapeDtypeStruct((b, Q_W, seq), F32)
    return pl.pallas_call(
        functools.partial(_attn_prompt_kernel, seq=seq),
        grid=(b, N_KV, seq // TQ),
        in_specs=[pl.BlockSpec(memory_space=pltpu.SMEM), q_spec(0), q_spec(N_KV),
                  k_spec(0), k_spec(1), k_spec(2), v_spec(3), v_spec(7), v_spec(5),
                  pl.BlockSpec((1, 1, n_cmp, HEAD_DIM), lambda bi, g, qt: (bi, g, 0, 0)),
                  pl.BlockSpec((1, 1, HEAD_DIM, n_cmp), lambda bi, g, qt: (bi, g, 0, 0)),
                  pl.BlockSpec((1, GATE_PAD, TQ), lambda bi, g, qt: (bi, g, qt))],
        out_specs=[o_spec, o_spec],
        out_shape=[o_shape, o_shape],
        scratch_shapes=[pltpu.VMEM((3, seq, 2 * HEAD_DIM), BF16), pltpu.VMEM((seq // MOBA_BLOCK, HEAD_DIM), F32)],
        compiler_params=_cparams("parallel", "parallel", "arbitrary"),
        name="attn_prompt",
    )(slopes, q_fm, q_fm, k_tm, k_tm, k_tm, kv_fm, kv_fm, kv_fm, k_cmp, v_cmp_t, gt_fm)


def _flash_update(carry, q, k, v, bias):
    m, l, acc = carry
    s = _dot_t(q, k) + bias
    m_new = jnp.maximum(m, jnp.max(s, axis=-1, keepdims=True))
    alpha = jnp.exp(m - m_new)
    p = jnp.exp(s - m_new)
    l = alpha * l + jnp.sum(p, axis=-1, keepdims=True)
    return m_new, l, alpha * acc + _dot(p.astype(BF16), v)


def _diag_blocks(o, rows_per_group):
    return jnp.concatenate(
        [o[g * rows_per_group:(g + 1) * rows_per_group, g * HEAD_DIM:(g + 1) * HEAD_DIM] for g in range(N_KV)], axis=0)


def _page_spec(layer, n_pool, n_pages, pages_per_step, i):
    return pl.BlockSpec((1, KV_W, PAGE_SIZE),
                        lambda bi, s, pt: (layer * n_pool + pt[bi * n_pages + s * pages_per_step + i], 0, 0))


def _block_sum_kernel(pt_ref, *refs, pages_per_step):
    del pt_ref
    o_ref = refs[pages_per_step]
    step = pl.program_id(1)

    @pl.when(step == 0)
    def _():
        o_ref[...] = jnp.zeros(o_ref.shape, F32)

    lane = lax.broadcasted_iota(jnp.int32, (KV_W, o_ref.shape[2]), 1)
    acc = o_ref[0]
    for i in range(pages_per_step):
        blk = _div_pow2(step * pages_per_step + i, MOBA_BLOCK // PAGE_SIZE)
        acc = acc + jnp.where(lane == blk, jnp.sum(refs[i][0], axis=1, keepdims=True), 0.0)
    o_ref[0] = acc


def moba_block_sums(pool, layer, page_table, pages_per_step, n_lanes):
    b, n_pages = page_table.shape
    n_pool = pool.shape[0] // DEPTH
    assert n_pages % pages_per_step == 0 and n_pages * PAGE_SIZE // MOBA_BLOCK <= n_lanes
    return pl.pallas_call(
        functools.partial(_block_sum_kernel, pages_per_step=pages_per_step),
        grid_spec=pltpu.PrefetchScalarGridSpec(
            num_scalar_prefetch=1,
            grid=(b, n_pages // pages_per_step),
            in_specs=[_page_spec(layer, n_pool, n_pages, pages_per_step, i) for i in range(pages_per_step)],
            out_specs=pl.BlockSpec((1, KV_W, n_lanes), lambda bi, s, pt: (bi, 0, 0)),
        ),
        out_shape=jax.ShapeDtypeStruct((b, KV_W, n_lanes), F32),
        compiler_params=_cparams("parallel", "arbitrary"),
        name="moba_block_sums",
    )(page_table.reshape(-1), *([pool] * pages_per_step))


def _cmp_pages_kernel(pt_ref, *refs, n_pages):
    del pt_ref
    page_refs = refs[:n_pages]
    pos_ref, w1_ref, w2_ref, o_ref, buf_ref = refs[n_pages:]
    blocks_per_page = PAGE_SIZE // CMP_BLOCK
    hid = w2_ref.shape[0]
    for i in range(n_pages):
        buf_ref[i * KV_W:(i + 1) * KV_W, :] = page_refs[i][0]
    w2 = w2_ref[...]
    for g in range(N_KV):

        def body(j, acc, g=g):
            pieces = []
            for t in range(CMP_DIMS_PER_DOT):
                d = j * CMP_DIMS_PER_DOT + t
                x = buf_ref[pl.ds(g * HEAD_DIM + d, n_pages, stride=KV_W), :]
                x = (x + pos_ref[pl.ds(d, 1), :]).astype(BF16)
                pieces.append(jnp.concatenate([x[:, i * CMP_BLOCK:(i + 1) * CMP_BLOCK] for i in range(blocks_per_page)], axis=0))
            return acc + _dot(jnp.concatenate(pieces, axis=1), w1_ref[j])

        h = lax.fori_loop(0, HEAD_DIM // CMP_DIMS_PER_DOT, body, jnp.zeros((blocks_per_page * n_pages, hid), F32), unroll=4)
        h = (h * jax.nn.sigmoid(h)).astype(BF16)
        y = _dot(h, w2)
        for blk in range(blocks_per_page):
            o_ref[0, blk, :, g * HEAD_DIM:(g + 1) * HEAD_DIM] = y[blk * n_pages:(blk + 1) * n_pages, :]


def cmp_pages(pool, layer, page_table, pos_t, w1_by_dim, w2):
    b, n_pages = page_table.shape
    n_pool = pool.shape[0] // DEPTH
    blocks_per_page = PAGE_SIZE // CMP_BLOCK
    page_spec = lambda i: pl.BlockSpec((1, KV_W, PAGE_SIZE), lambda bi, pt: (layer * n_pool + pt[bi * n_pages + i], 0, 0))
    full = lambda a: pl.BlockSpec(a.shape, lambda bi, pt: (0,) * a.ndim)
    return pl.pallas_call(
        functools.partial(_cmp_pages_kernel, n_pages=n_pages),
        grid_spec=pltpu.PrefetchScalarGridSpec(
            num_scalar_prefetch=1,
            grid=(b,),
            in_specs=[page_spec(i) for i in range(n_pages)] + [full(pos_t), full(w1_by_dim), full(w2)],
            out_specs=pl.BlockSpec((1, blocks_per_page, n_pages, KV_W), lambda bi, pt: (bi, 0, 0, 0)),
            scratch_shapes=[pltpu.VMEM((n_pages * KV_W, PAGE_SIZE), F32)],
        ),
        out_shape=jax.ShapeDtypeStruct((b, blocks_per_page, n_pages, KV_W), F32),
        compiler_params=_cparams("parallel"),
        name="cmp_pages",
    )(page_table.reshape(-1), *([pool] * n_pages), pos_t, w1_by_dim, w2)


def _cmp_w1_by_dim(w1):
    w = w1.reshape(CMP_BLOCK, HEAD_DIM, w1.shape[1]).transpose(1, 0, 2)
    return w.reshape(HEAD_DIM // CMP_DIMS_PER_DOT, CMP_DIMS_PER_DOT * CMP_BLOCK, w1.shape[1]).astype(BF16)


def _sample_select_kernel(qn_ref, qm_ref, kc_ref, vc_ref, kmean_ref, slope_ref, tpos_ref,
                          ocmp_ref, sbias_ref, mbias_ref, *, past, n_new):
    rows = N_HEADS * n_new
    rpg = N_REP * n_new
    n_cmp = kc_ref.shape[1]
    n_sel_lanes = sbias_ref.shape[2]
    n_moba_lanes = mbias_ref.shape[2]
    slope = slope_ref[...]
    qpos = past + tpos_ref[...]

    blk = lax.broadcasted_iota(jnp.int32, (rows, n_cmp), 1)
    dist = qpos - (blk * CMP_BLOCK + (CMP_BLOCK - 1))
    p = _masked_softmax(_dot_t(qn_ref[0], kc_ref[0].astype(BF16)) - slope * dist.astype(F32), dist >= 0, -1)
    ocmp_ref[0] = _diag_blocks(_dot(p.astype(BF16), vc_ref[0].astype(BF16)), rpg)

    imp = jnp.sum(p.reshape(N_KV, N_REP, n_new, n_cmp), axis=1).reshape(N_KV * n_new, n_cmp)
    imp = jnp.concatenate([imp, jnp.zeros((N_KV * n_new, n_sel_lanes - n_cmp), F32)], axis=1)
    sblk = lax.broadcasted_iota(jnp.int32, (N_KV * n_new, n_sel_lanes), 1)
    assert n_new & (n_new - 1) == 0
    spos = past + (lax.broadcasted_iota(jnp.int32, (N_KV * n_new, 1), 0) & (n_new - 1))
    n_sel_blocks = -(-(past + n_new) // SEL_BLOCK)
    visible = (sblk * SEL_BLOCK + (SEL_BLOCK - 1)) <= spos
    forced = (sblk == _div_pow2(spos, SEL_BLOCK)) | (sblk == 0)
    score = jnp.where(forced, jnp.inf, jnp.where(visible, imp, -jnp.inf))
    score = jnp.where(sblk < n_sel_blocks, score, -jnp.inf)
    sb = _topk_bias(score, N_SEL, sblk, -1)
    sb = jnp.broadcast_to(sb.reshape(N_KV, 1, n_new, n_sel_lanes), (N_KV, N_REP, n_new, n_sel_lanes))
    sbias_ref[0] = sb.reshape(rows, n_sel_lanes)

    mblk = lax.broadcasted_iota(jnp.int32, (rows, n_moba_lanes), 1)
    own = _div_pow2(qpos, MOBA_BLOCK)
    gate = _dot(qm_ref[0], kmean_ref[0].astype(BF16))
    mscore = jnp.where(mblk == own, jnp.inf, jnp.where(mblk < own, gate, -jnp.inf))
    mbias_ref[0] = _topk_bias(mscore, MOBA_TOPK + 1, mblk, -1)


def sample_select(qn, qm, kc, vc, kmean_t, slope, tpos, past, n_new, n_sel_lanes):
    b, rows, _ = qn.shape

    def spec(a):
        return pl.BlockSpec((1,) + a.shape[1:], lambda bi: (bi,) + (0,) * (a.ndim - 1))

    def full(a):
        return pl.BlockSpec(a.shape, lambda bi: (0,) * a.ndim)

    outs = [jax.ShapeDtypeStruct((b, rows, HEAD_DIM), F32),
            jax.ShapeDtypeStruct((b, rows, n_sel_lanes), F32),
            jax.ShapeDtypeStruct((b, rows, kmean_t.shape[2]), F32)]
    return pl.pallas_call(
        functools.partial(_sample_select_kernel, past=past, n_new=n_new),
        grid=(b,),
        in_specs=[spec(qn), spec(qm), spec(kc), spec(vc), spec(kmean_t), full(slope), full(tpos)],
        out_specs=[spec(o) for o in outs],
        out_shape=outs,
        compiler_params=_cparams("parallel"),
        name="sample_select",
    )(qn, qm, kc, vc, kmean_t, slope, tpos)


def _attn_sample_kernel(pt_ref, *refs, pages_per_step, past, n_new):
    del pt_ref
    pps = pages_per_step
    (qn_ref, qm_ref, sbias_ref, mbias_ref, slope_ref, tpos_ref) = refs[:6]
    page_refs = refs[6:6 + 4 * pps]
    (ksn_ref, vsn_ref, kmn_ref, vmn_ref, kw_ref, vw_ref, ocmp_ref, gt_ref) = refs[6 + 4 * pps:14 + 4 * pps]
    on_ref, om_ref = refs[14 + 4 * pps:16 + 4 * pps]
    ms_ref, ls_ref, as_ref, mm_ref, lm_ref, am_ref = refs[16 + 4 * pps:]
    step = pl.program_id(1)
    rpg = N_REP * n_new
    qn = qn_ref[0]
    qm = qm_ref[0]
    slope = slope_ref[...]
    tpos = tpos_ref[...]

    @pl.when(step == 0)
    def _():
        ms_ref[...] = jnp.full(ms_ref.shape, NEG_BIG, F32)
        mm_ref[...] = jnp.full(mm_ref.shape, NEG_BIG, F32)
        ls_ref[...] = jnp.zeros(ls_ref.shape, F32)
        lm_ref[...] = jnp.zeros(lm_ref.shape, F32)
        as_ref[...] = jnp.zeros(as_ref.shape, F32)
        am_ref[...] = jnp.zeros(am_ref.shape, F32)

    sbias = sbias_ref[0].astype(BF16)
    mbias = mbias_ref[0].astype(BF16)
    n_keys = pps * PAGE_SIZE
    first_key = step * n_keys
    key = lax.broadcasted_iota(jnp.int32, (1, n_keys), 1)
    alibi = slope * (first_key - past + key).astype(F32)
    s_blk = _div_pow2(first_key + lax.broadcasted_iota(jnp.int32, (sbias.shape[1], n_keys), 1), SEL_BLOCK)
    s_expand = (lax.broadcasted_iota(jnp.int32, (sbias.shape[1], n_keys), 0) == s_blk).astype(BF16)
    m_blk = _div_pow2(first_key + lax.broadcasted_iota(jnp.int32, (mbias.shape[1], n_keys), 1), MOBA_BLOCK)
    m_expand = (lax.broadcasted_iota(jnp.int32, (mbias.shape[1], n_keys), 0) == m_blk).astype(BF16)

    def update(carry, q, k_refs, v_refs, bias):
        m, l, acc = carry
        s = jnp.concatenate([_dot(q, k_ref[0].astype(BF16)) for k_ref in k_refs], axis=1) + bias
        m_new = jnp.maximum(m, jnp.max(s, axis=-1, keepdims=True))
        alpha = jnp.exp(m - m_new)
        p = jnp.exp(s - m_new)
        l = alpha * l + jnp.sum(p, axis=-1, keepdims=True)
        acc = alpha * acc
        for i, v_ref in enumerate(v_refs):
            acc = acc + _dot_t(p[:, i * PAGE_SIZE:(i + 1) * PAGE_SIZE].astype(BF16), v_ref[0].astype(BF16))
        return m_new, l, acc

    slc = update((ms_ref[...], ls_ref[...], as_ref[...]), qn, page_refs[:pps], page_refs[pps:2 * pps],
                 alibi + _dot(sbias, s_expand))
    moba = update((mm_ref[...], lm_ref[...], am_ref[...]), qm, page_refs[2 * pps:3 * pps], page_refs[3 * pps:],
                  alibi + _dot(mbias, m_expand))
    ms_ref[...], ls_ref[...], as_ref[...] = slc
    mm_ref[...], lm_ref[...], am_ref[...] = moba

    @pl.when(step == pl.num_programs(1) - 1)
    def _():
        new = lax.broadcasted_iota(jnp.int32, (1, n_new), 1)
        bias = jnp.where(new <= tpos, slope * new.astype(F32), NEG_BIG)
        _, l, acc = _flash_update(slc, qn, ksn_ref[0].astype(BF16), vsn_ref[0].astype(BF16), bias)
        o_slc = _diag_blocks(acc / l, rpg)
        _, l, acc = _flash_update(moba, qm, kmn_ref[0].astype(BF16), vmn_ref[0].astype(BF16), bias)
        om_ref[0] = _diag_blocks(acc / l, rpg)

        n_ctx = kw_ref.shape[2]
        wrel = lax.broadcasted_iota(jnp.int32, (1, n_ctx), 1) - (n_ctx - n_new)
        dist = tpos - wrel
        valid = (dist >= 0) & (dist < WINDOW) & (wrel + past >= 0)
        s = jnp.where(valid, _dot(qn, kw_ref[0].astype(BF16)) + slope * wrel.astype(F32), NEG_BIG)
        pw = jnp.exp(s - jnp.max(s, axis=-1, keepdims=True))
        o_win = _diag_blocks(_dot_t(pw.astype(BF16), vw_ref[0].astype(BF16)) / jnp.sum(pw, axis=-1, keepdims=True), rpg)

        gates = jax.nn.sigmoid(gt_ref[0])
        on_ref[0] = gates[:, 0:1] * ocmp_ref[0] + gates[:, 1:2] * o_slc + gates[:, 2:3] * o_win


def attn_sample(page_table, layer, qn, qm, sbias, mbias, slope, tpos, pools, new_kv, kwin_t, vwin_t, ocmp, gt,
                past, n_new, pages_per_step):
    b, rows, _ = qn.shape
    n_pages = page_table.shape[1]
    n_pool = pools[0].shape[0] // DEPTH
    pps = pages_per_step
    assert n_pages % pps == 0 and n_pages * PAGE_SIZE == past

    def spec(a):
        return pl.BlockSpec((1,) + a.shape[1:], lambda bi, s, pt: (bi,) + (0,) * (a.ndim - 1))

    def full(a):
        return pl.BlockSpec(a.shape, lambda bi, s, pt: (0,) * a.ndim)

    page_specs = [_page_spec(layer, n_pool, n_pages, pps, i) for _ in range(4) for i in range(pps)]
    page_args = [pool for pool in pools for _ in range(pps)]
    o_shape = jax.ShapeDtypeStruct((b, rows, HEAD_DIM), F32)
    return pl.pallas_call(
        functools.partial(_attn_sample_kernel, pages_per_step=pps, past=past, n_new=n_new),
        grid_spec=pltpu.PrefetchScalarGridSpec(
            num_scalar_prefetch=1,
            grid=(b, n_pages // pps),
            in_specs=[spec(qn), spec(qm), spec(sbias), spec(mbias), full(slope), full(tpos)] + page_specs
                     + [spec(a) for a in new_kv] + [spec(kwin_t), spec(vwin_t), spec(ocmp), spec(gt)],
            out_specs=[spec(o_shape), spec(o_shape)],
            scratch_shapes=[pltpu.VMEM((rows, 1), F32), pltpu.VMEM((rows, 1), F32), pltpu.VMEM((rows, KV_W), F32),
                            pltpu.VMEM((rows, 1), F32), pltpu.VMEM((rows, 1), F32), pltpu.VMEM((rows, KV_W), F32)],
        ),
        out_shape=[o_shape, o_shape],
        compiler_params=_cparams("parallel", "arbitrary"),
        name="attn_sample",
    )(page_table.reshape(-1), qn, qm, sbias, mbias, slope, tpos, *page_args, *new_kv, kwin_t, vwin_t, ocmp, gt)


def _out_proj(o_nsa, o_moba, gn_ref, gm_ref, wn_ref, wm_ref, x_ref, g_ref, b_ref, o_ref):
    hn = _rms_norm(o_nsa, gn_ref[...]).astype(BF16)
    hm = _rms_norm(o_moba, gm_ref[...]).astype(BF16)
    y = _dot(hn, wn_ref[...]) + _dot(hm, wm_ref[...])
    o_ref[...] = _layer_norm(DEEPNORM_ALPHA * x_ref[...] + y, g_ref[...], b_ref[...])


def _out_proj_kernel(on_ref, om_ref, *refs):
    _out_proj(on_ref[...], om_ref[...], *refs)


def _out_proj_fm_kernel(on_ref, om_ref, *refs):
    _out_proj(on_ref[0].T, om_ref[0].T, *refs)


def out_proj_ln(o_nsa, o_moba, gn_nsa, gn_moba, w_nsa, w_moba, x, g, b, feature_major):
    m = x.shape[0]
    assert m % ROW_TILE == 0
    row = lambda w: pl.BlockSpec((ROW_TILE, w), lambda i: (i, 0))
    full = lambda a: pl.BlockSpec(a.shape, lambda i: (0,) * a.ndim)
    if feature_major:
        per_batch = o_nsa.shape[2] // ROW_TILE
        o_spec = pl.BlockSpec((1, Q_W, ROW_TILE), lambda i: (i // per_batch, 0, i % per_batch))
    else:
        o_spec = row(Q_W)
    return pl.pallas_call(
        _out_proj_fm_kernel if feature_major else _out_proj_kernel,
        grid=(m // ROW_TILE,),
        in_specs=[o_spec, o_spec, full(gn_nsa), full(gn_moba), full(w_nsa), full(w_moba),
                  row(D_MODEL), full(g), full(b)],
        out_specs=row(D_MODEL),
        out_shape=jax.ShapeDtypeStruct((m, D_MODEL), F32),
        compiler_params=_cparams("parallel"),
        name="out_proj_ln",
    )(o_nsa, o_moba, gn_nsa, gn_moba, w_nsa, w_moba, x, g, b)


def _mem_attn_kernel(x_ref, wq_ref, k_ref, v_ref, wo_ref, g_ref, b_ref, o_ref):
    x = x_ref[0]
    q = _dot(x.astype(BF16), wq_ref[...]).astype(BF16)
    k = k_ref[0].astype(BF16)
    v = v_ref[0].astype(BF16)
    heads = []
    for h in range(MEM_HEADS):
        sl = slice(h * MEM_HEAD_DIM, (h + 1) * MEM_HEAD_DIM)
        s = _dot_t(q[:, sl], k[:, sl]) * (MEM_HEAD_DIM ** -0.5)
        e = jnp.exp(s - jnp.max(s, axis=-1, keepdims=True))
        p = e / jnp.sum(e, axis=-1, keepdims=True)
        heads.append(_dot(p.astype(BF16), v[:, sl]))
    o = jnp.concatenate(heads, axis=-1).astype(BF16)
    o_ref[0] = _layer_norm(DEEPNORM_ALPHA * x + _dot(o, wo_ref[...]), g_ref[...], b_ref[...])


def mem_attn_ln(x, wq, mem_k, mem_v, wo, g, b, tq):
    bsz, t, d = x.shape
    n_mem, d_mem = mem_k.shape[1:]
    assert t % tq == 0
    full = lambda a: pl.BlockSpec(a.shape, lambda bi, i: (0,) * a.ndim)
    return pl.pallas_call(
        _mem_attn_kernel,
        grid=(bsz, t // tq),
        in_specs=[pl.BlockSpec((1, tq, d), lambda bi, i: (bi, i, 0)), full(wq),
                  pl.BlockSpec((1, n_mem, d_mem), lambda bi, i: (bi, 0, 0)),
                  pl.BlockSpec((1, n_mem, d_mem), lambda bi, i: (bi, 0, 0)), full(wo), full(g), full(b)],
        out_specs=pl.BlockSpec((1, tq, d), lambda bi, i: (bi, i, 0)),
        out_shape=jax.ShapeDtypeStruct(x.shape, F32),
        compiler_params=_cparams("parallel", "arbitrary"),
        name="mem_attn_ln",
    )(x, wq, mem_k, mem_v, wo, g, b)


def _router_kernel(x_ref, w_ref, b_ref, e_ref, g_ref):
    logits = jnp.dot(x_ref[...], w_ref[...], preferred_element_type=F32, precision=lax.Precision.HIGHEST) + b_ref[...]
    lane = lax.broadcasted_iota(jnp.int32, logits.shape, 1).astype(F32)
    cur = logits
    vals, idxs = [], []
    for _ in range(TOP_K):
        m = jnp.max(cur, axis=-1, keepdims=True)
        idx = jnp.min(jnp.where(cur == m, lane, float(N_EXPERTS)), axis=-1, keepdims=True)
        vals.append(m)
        idxs.append(idx)
        cur = jnp.where(lane == idx, -jnp.inf, cur)
    top_v = jnp.concatenate(vals, axis=-1)
    e = jnp.exp(top_v - vals[0])
    g_ref[...] = e / jnp.sum(e, axis=-1, keepdims=True)
    e_ref[...] = jnp.concatenate(idxs, axis=-1).astype(jnp.int32)


def router(x, w, b):
    m, d = x.shape
    assert m % ROW_TILE == 0
    return pl.pallas_call(
        _router_kernel,
        grid=(m // ROW_TILE,),
        in_specs=[pl.BlockSpec((ROW_TILE, d), lambda i: (i, 0)),
                  pl.BlockSpec(w.shape, lambda i: (0, 0)), pl.BlockSpec(b.shape, lambda i: (0, 0))],
        out_specs=[pl.BlockSpec((ROW_TILE, TOP_K), lambda i: (i, 0)), pl.BlockSpec((ROW_TILE, TOP_K), lambda i: (i, 0))],
        out_shape=[jax.ShapeDtypeStruct((m, TOP_K), jnp.int32), jax.ShapeDtypeStruct((m, TOP_K), F32)],
        compiler_params=_cparams("parallel"),
        name="router",
    )(x, w, b)


def _dispatch_kernel(nu_ref, src_ref, src_next_ref, x_hbm, o_ref, buf_ref, sem_ref):
    i = pl.program_id(0)
    n_used = nu_ref[0]
    slot = i & 1

    def row_copy(row, r, slot):
        return pltpu.make_async_copy(x_hbm.at[pl.ds(row, 1), :], buf_ref.at[slot, pl.ds(r, 1), :], sem_ref.at[slot])

    def start_gather(rows_ref, slot):
        def body(r, carry):
            row_copy(rows_ref[0, 0, r], r, slot).start()
            return carry

        lax.fori_loop(0, MOE_TILE, body, 0, unroll=8)

    @pl.when((i == 0) & (n_used > 0))
    def _():
        start_gather(src_ref, 0)

    @pl.when(i + 1 < n_used)
    def _():
        start_gather(src_next_ref, 1 - slot)

    @pl.when(i < n_used)
    def _():
        def body(r, carry):
            row_copy(0, r, slot).wait()
            return carry

        lax.fori_loop(0, MOE_TILE, body, 0, unroll=8)
        o_ref[...] = buf_ref[slot].astype(BF16)

    @pl.when(i >= n_used)
    def _():
        o_ref[...] = jnp.zeros(o_ref.shape, BF16)


def dispatch_rows(x, src, n_used):
    n, d = x.shape
    n_blocks = src.shape[0] // MOE_TILE
    src3 = src.reshape(n_blocks, 1, MOE_TILE)
    return pl.pallas_call(
        _dispatch_kernel,
        grid_spec=pltpu.PrefetchScalarGridSpec(
            num_scalar_prefetch=1,
            grid=(n_blocks,),
            in_specs=[pl.BlockSpec((1, 1, MOE_TILE), lambda i, nu: (i, 0, 0), memory_space=pltpu.SMEM),
                      pl.BlockSpec((1, 1, MOE_TILE), lambda i, nu: (jnp.minimum(i + 1, n_blocks - 1), 0, 0),
                                   memory_space=pltpu.SMEM),
                      pl.BlockSpec(memory_space=pl.ANY)],
            out_specs=pl.BlockSpec((MOE_TILE, d), lambda i, nu: (i, 0)),
            scratch_shapes=[pltpu.VMEM((2, MOE_TILE, d), F32), pltpu.SemaphoreType.DMA((2,))],
        ),
        out_shape=jax.ShapeDtypeStruct((n_blocks * MOE_TILE, d), BF16),
        compiler_params=_cparams("arbitrary"),
        name="dispatch_rows",
    )(n_used, src3, src3, x)


def _expert_up_kernel(be_ref, nu_ref, x_ref, wg_ref, wu_ref, bg_ref, bu_ref, h_ref):
    del be_ref

    @pl.when(pl.program_id(1) < nu_ref[0])
    def _():
        x = x_ref[...]
        g = _dot(x, wg_ref[0].astype(BF16)) + bg_ref[0]
        u = _dot(x, wu_ref[0].astype(BF16)) + bu_ref[0]
        g = jnp.minimum(g, SWIGLU_LIMIT)
        u = jnp.clip(u, -SWIGLU_LIMIT, SWIGLU_LIMIT)
        h_ref[...] = (g * jax.nn.sigmoid(SWIGLU_ALPHA * g) * (u + 1.0)).astype(BF16)

    @pl.when(pl.program_id(1) >= nu_ref[0])
    def _():
        h_ref[...] = jnp.zeros(h_ref.shape, BF16)


def _expert_down_kernel(be_ref, nu_ref, h_ref, wd_ref, bd_ref, y_ref):
    del be_ref

    @pl.when(pl.program_id(1) < nu_ref[0])
    def _():
        y_ref[...] = _dot(h_ref[...], wd_ref[0].astype(BF16)) + bd_ref[0]

    @pl.when(pl.program_id(1) >= nu_ref[0])
    def _():
        y_ref[...] = jnp.zeros(y_ref.shape, F32)


def expert_ffn(xs, blk_e, n_used, layer, wg, bg, wu, bu, wd, bd):
    cap, d = xs.shape
    n_blocks = cap // MOE_TILE
    d_ff = wg.shape[-1]
    we = lambda j, i, be, nu: layer * N_EXPERTS + be[i]
    h = pl.pallas_call(
        _expert_up_kernel,
        grid_spec=pltpu.PrefetchScalarGridSpec(
            num_scalar_prefetch=2,
            grid=(d_ff // FF_CHUNK, n_blocks),
            in_specs=[pl.BlockSpec((MOE_TILE, d), lambda j, i, be, nu: (i, 0)),
                      pl.BlockSpec((1, d, FF_CHUNK), lambda j, i, be, nu: (we(j, i, be, nu), 0, j)),
                      pl.BlockSpec((1, d, FF_CHUNK), lambda j, i, be, nu: (we(j, i, be, nu), 0, j)),
                      pl.BlockSpec((1, 1, FF_CHUNK), lambda j, i, be, nu: (we(j, i, be, nu), 0, j)),
                      pl.BlockSpec((1, 1, FF_CHUNK), lambda j, i, be, nu: (we(j, i, be, nu), 0, j))],
            out_specs=pl.BlockSpec((MOE_TILE, FF_CHUNK), lambda j, i, be, nu: (i, j)),
        ),
        out_shape=jax.ShapeDtypeStruct((cap, d_ff), BF16),
        compiler_params=_cparams("parallel", "arbitrary"),
        name="expert_up",
    )(blk_e, n_used, xs, wg, wu, bg, bu)
    return pl.pallas_call(
        _expert_down_kernel,
        grid_spec=pltpu.PrefetchScalarGridSpec(
            num_scalar_prefetch=2,
            grid=(d // DOWN_CHUNK, n_blocks),
            in_specs=[pl.BlockSpec((MOE_TILE, d_ff), lambda j, i, be, nu: (i, 0)),
                      pl.BlockSpec((1, d_ff, DOWN_CHUNK), lambda j, i, be, nu: (we(j, i, be, nu), 0, j)),
                      pl.BlockSpec((1, 1, DOWN_CHUNK), lambda j, i, be, nu: (we(j, i, be, nu), 0, j))],
            out_specs=pl.BlockSpec((MOE_TILE, DOWN_CHUNK), lambda j, i, be, nu: (i, j)),
        ),
        out_shape=jax.ShapeDtypeStruct((cap, d), F32),
        compiler_params=_cparams("parallel", "arbitrary"),
        name="expert_down",
    )(blk_e, n_used, h, wd, bd)


def _residual_ln_kernel(x_ref, m_ref, g_ref, b_ref, o_ref):
    o_ref[...] = _layer_norm(DEEPNORM_ALPHA * x_ref[...] + m_ref[...], g_ref[...], b_ref[...])


def residual_ln(x, m, g, b):
    n, d = x.shape
    row = pl.BlockSpec((ROW_TILE, d), lambda i: (i, 0))
    full = lambda a: pl.BlockSpec(a.shape, lambda i: (0,) * a.ndim)
    return pl.pallas_call(
        _residual_ln_kernel,
        grid=(n // ROW_TILE,),
        in_specs=[row, row, full(g), full(b)],
        out_specs=row,
        out_shape=jax.ShapeDtypeStruct((n, d), F32),
        compiler_params=_cparams("parallel"),
        name="residual_ln",
    )(x, m, g, b)


def moe_ffn(x, layer, router_w, router_b, wg, bg, wu, bu, wd, bd):
    n, d = x.shape
    top_e, gate = router(x, router_w, router_b)
    a = n * TOP_K
    e_flat = top_e.reshape(a)
    onehot = (e_flat[:, None] == jnp.arange(N_EXPERTS, dtype=jnp.int32)[None, :]).astype(jnp.int32)
    csum = jnp.cumsum(onehot, axis=0)
    rank = jnp.sum((csum - onehot) * onehot, axis=1)
    counts = csum[-1]
    padded = (counts + MOE_TILE - 1) // MOE_TILE * MOE_TILE
    pad_end = jnp.cumsum(padded)
    pad_start = pad_end - padded
    dest = pad_start[e_flat] + rank
    n_blocks = -(-a // MOE_TILE) + N_EXPERTS
    cap = n_blocks * MOE_TILE
    tok = jnp.arange(a, dtype=jnp.int32) // TOP_K
    src = jnp.zeros((cap,), jnp.int32).at[dest].set(tok)
    blk_start = jnp.arange(n_blocks, dtype=jnp.int32) * MOE_TILE
    blk_e = jnp.minimum(jnp.sum(blk_start[:, None] >= pad_end[None, :], axis=1), N_EXPERTS - 1).astype(jnp.int32)
    n_used = (pad_end[-1] // MOE_TILE).astype(jnp.int32).reshape(1)
    xs = dispatch_rows(x, src, n_used)
    y = expert_ffn(xs, blk_e, n_used, layer, wg, bg, wu, bu, wd, bd)
    return jnp.sum(y[dest.reshape(n, TOP_K).T] * gate.T[:, :, None], axis=0)


def _alibi_slopes():
    return jnp.asarray(2.0 ** (-8.0 * np.arange(1, N_HEADS + 1) / N_HEADS), F32)


def _split_w_in(w):
    q_n, kv_n, gt, q_m, kv_m = jnp.split(w, [Q_W, Q_W + 6 * KV_W, Q_W + 6 * KV_W + 3 * N_HEADS,
                                            2 * Q_W + 6 * KV_W + 3 * N_HEADS], axis=1)
    kc, vc, ks, vs, kw, vw = jnp.split(kv_n, 6, axis=1)
    km, vm = jnp.split(kv_m, 2, axis=1)
    return q_n, (kc, vc, ks, vs, kw, vw, km, vm), gt, q_m


def _block_diag_rows(q, b, t):
    q = (q * (HEAD_DIM ** -0.5)).astype(BF16).reshape(b, t, N_KV, N_REP, HEAD_DIM).transpose(0, 2, 3, 1, 4)
    eye = jnp.eye(N_KV, dtype=BF16)
    bd = q[:, :, :, :, None, :] * eye[None, :, None, None, :, None]
    return bd.reshape(b, N_HEADS * t, KV_W)


def _rows_to_tokens(o, b, t):
    return o.reshape(b, N_KV, N_REP, t, HEAD_DIM).transpose(0, 3, 1, 2, 4).reshape(b * t, Q_W)


def _token_minor(a):
    lead = a.shape[:-2]
    a = a.reshape(lead + (N_KV, HEAD_DIM, a.shape[-1]))
    n = len(lead)
    return a.transpose(tuple(range(n)) + (n + 2, n, n + 1))


def kernel(x_prompt, x_sample, cache_nsa_cmp_k, cache_nsa_cmp_v, cache_nsa_slc_k, cache_nsa_slc_v, cache_moba_k, cache_moba_v, cache_nsa_win_k, cache_nsa_win_v, cache_mem_k, cache_mem_v, page_table, mem_prompt, w_in, nsa_cmp_pos, nsa_cmp_k_w1, nsa_cmp_k_w2, nsa_cmp_v_w1, nsa_cmp_v_w2, gn_nsa, gn_moba, w_out, ln1_g, ln1_b, ca_wq, ca_wk, ca_wv, ca_wo, ln2_g, ln2_b, router_w, router_b, exp_wg, exp_bg, exp_wu, exp_bu, exp_wd, exp_bd, ln3_g, ln3_b):
    bp, tp, d = x_prompt.shape
    bs, ts, _ = x_sample.shape
    n_p, n_s = bp * tp, bs * ts
    n_mem = mem_prompt.shape[1]
    depth, n_pool = cache_nsa_cmp_k.shape[:2]
    n_pages = page_table.shape[1]
    past = n_pages * PAGE_SIZE
    wbuf = cache_nsa_win_k.shape[2]
    d_mem = MEM_HEADS * MEM_HEAD_DIM
    slopes = _alibi_slopes()
    fm = lambda c: c.transpose(0, 1, 3, 4, 2).reshape(c.shape[0], c.shape[1], KV_W, c.shape[2])
    pool = lambda c: fm(c).reshape(depth * n_pool, KV_W, PAGE_SIZE)
    pools = tuple(map(pool, (cache_nsa_slc_k, cache_nsa_slc_v, cache_moba_k, cache_moba_v)))
    cmp_pools = tuple(map(pool, (cache_nsa_cmp_k, cache_nsa_cmp_v)))
    win_k_fm, win_v_fm = fm(cache_nsa_win_k), fm(cache_nsa_win_v)
    wg = exp_wg.reshape((depth * N_EXPERTS,) + exp_wg.shape[2:])
    wu = exp_wu.reshape((depth * N_EXPERTS,) + exp_wu.shape[2:])
    wd = exp_wd.reshape((depth * N_EXPERTS,) + exp_wd.shape[2:])
    bg = exp_bg.reshape(depth * N_EXPERTS, 1, -1)
    bu = exp_bu.reshape(depth * N_EXPERTS, 1, -1)
    bd = exp_bd.reshape(depth * N_EXPERTS, 1, -1)

    row = np.arange(N_HEADS * ts)
    s_slope = slopes[row // ts].reshape(-1, 1)
    s_tpos = jnp.asarray((row % ts).reshape(-1, 1), jnp.int32)
    n_cmp_s = (past + ts) // CMP_BLOCK
    n_sel_lanes = -(-(-(-(past + ts) // SEL_BLOCK)) // 128) * 128
    n_moba_past = past // MOBA_BLOCK

    x_p = x_prompt
    x_s = x_sample.reshape(n_s, d)
    p_states, s_states, mem_states = [], [], []
    s_win = []
    for l in range(depth):
        q_n, kv, gt, q_m = _split_w_in(w_in[l])
        kc, vc, ks, vs, kw, vw, km, vm = kv
        pos = nsa_cmp_pos[l].reshape(1, CMP_BLOCK * HEAD_DIM)
        ck_w1, ck_w2 = nsa_cmp_k_w1[l].astype(BF16), nsa_cmp_k_w2[l].astype(BF16)
        cv_w1, cv_w2 = nsa_cmp_v_w1[l].astype(BF16), nsa_cmp_v_w2[l].astype(BF16)

        wq_t = jnp.concatenate([q_n, q_m], axis=1).T.astype(BF16)
        wkv_t = jnp.concatenate([kc, vc, ks, vs, km, vm, kw, vw], axis=1).T.astype(BF16)
        wgt_t = jnp.pad(gt.reshape(d, N_KV, 3 * N_REP), ((0, 0), (0, 0), (0, GATE_PAD - 3 * N_REP)))
        wgt_t = wgt_t.reshape(d, N_KV * GATE_PAD).T.astype(BF16)
        wk_tm = jnp.concatenate([kc, vc, ks, kw, km], axis=1).astype(BF16)
        (q_fm,) = proj_feature_major(x_p, wq_t, (BF16,), HEAD_DIM ** -0.5)
        st_fm, kv_fm = proj_feature_major(x_p, wkv_t, (F32, BF16))
        (gt_fm,) = proj_feature_major(x_p, wgt_t, (F32,))
        kc_tm, vc_tm, k_tm = proj_keys_token_major(x_p, wk_tm)
        n_cmp_p = tp // CMP_BLOCK
        k_cmp = cmp_mlp(kc_tm.reshape(bp * N_KV * n_cmp_p, CMP_BLOCK * HEAD_DIM), pos, ck_w1, ck_w2, 256)
        v_cmp = cmp_mlp(vc_tm.reshape(bp * N_KV * n_cmp_p, CMP_BLOCK * HEAD_DIM), pos, cv_w1, cv_w2, 256)
        k_cmp = k_cmp.reshape(bp, N_KV, n_cmp_p, HEAD_DIM)
        v_cmp_t = v_cmp.reshape(bp, N_KV, n_cmp_p, HEAD_DIM).transpose(0, 1, 3, 2)
        o_nsa_p, o_moba_p = attn_prompt(slopes, q_fm, k_tm, kv_fm, k_cmp, v_cmp_t, gt_fm)

        w_main = jnp.concatenate([q_n, kc, vc, ks, vs, kw, vw, q_m, km, vm], axis=1).astype(BF16)
        w_gate = jnp.pad(gt, ((0, 0), (0, 128 - 3 * N_HEADS))).astype(BF16)
        proj_s = matmul(x_s, w_main, 256, 512)
        gt_s = matmul(x_s, w_gate, 256, 128)[:, :3 * N_HEADS]
        qs_n, kc_s, vc_s, ks_s, vs_s, kw_s, vw_s, qs_m, km_s, vm_s = jnp.split(
            proj_s, [Q_W + KV_W * i for i in range(7)] + [2 * Q_W + 6 * KV_W, 2 * Q_W + 7 * KV_W], axis=1)
        sn = lambda a: a.reshape(bs, ts, KV_W)

        pos_t = jnp.tile(nsa_cmp_pos[l].T, (1, PAGE_SIZE // CMP_BLOCK))
        blocks = lambda a: a.transpose(0, 2, 1, 3).reshape(bs, n_cmp_s, KV_W)
        k_cmp_s = blocks(cmp_pages(cmp_pools[0], l, page_table, pos_t, _cmp_w1_by_dim(nsa_cmp_k_w1[l]), ck_w2))
        v_cmp_s = blocks(cmp_pages(cmp_pools[1], l, page_table, pos_t, _cmp_w1_by_dim(nsa_cmp_v_w1[l]), cv_w2))
        k_mean_t = moba_block_sums(pools[2], l, page_table, min(SAMPLE_PAGES_PER_STEP, n_pages), 128)
        k_mean_t = k_mean_t.at[:, :, n_moba_past].set(sn(km_s).sum(axis=1)) * (1.0 / MOBA_BLOCK)
        qn_s = _block_diag_rows(qs_n, bs, ts)
        qm_s = _block_diag_rows(qs_m, bs, ts)
        o_cmp_s, sbias, mbias = sample_select(qn_s, qm_s, k_cmp_s, v_cmp_s, k_mean_t, s_slope, s_tpos, past, ts, n_sel_lanes)
        kwin_t = jnp.concatenate([win_k_fm[l], sn(kw_s).transpose(0, 2, 1)], axis=2)
        vwin_t = jnp.concatenate([win_v_fm[l], sn(vw_s).transpose(0, 2, 1)], axis=2)
        gt_rows = gt_s.reshape(bs, ts, N_KV, N_REP, 3).transpose(0, 2, 3, 1, 4).reshape(bs, N_HEADS * ts, 3)
        o_nsa_s, o_moba_s = attn_sample(
            page_table, l, qn_s, qm_s, sbias, mbias, s_slope, s_tpos, pools, (sn(ks_s), sn(vs_s), sn(km_s), sn(vm_s)),
            kwin_t, vwin_t, o_cmp_s, gt_rows, past, ts, min(SAMPLE_PAGES_PER_STEP, n_pages))

        w_o = w_out[l].astype(BF16)
        op_args = (gn_nsa[l].reshape(1, -1), gn_moba[l].reshape(1, -1), w_o[:Q_W], w_o[Q_W:])
        ln1 = (ln1_g[l].reshape(1, -1), ln1_b[l].reshape(1, -1))
        x_p = out_proj_ln(o_nsa_p, o_moba_p, *op_args, x_p.reshape(n_p, d), *ln1, True)
        x_s = out_proj_ln(_rows_to_tokens(o_nsa_s, bs, ts), _rows_to_tokens(o_moba_s, bs, ts), *op_args, x_s, *ln1, False)

        mem_kv = matmul(mem_prompt.reshape(bp * n_mem, d), jnp.concatenate([ca_wk[l], ca_wv[l]], axis=1).astype(BF16), 512, 512)
        mem_k, mem_v = mem_kv[:, :d_mem].reshape(bp, n_mem, d_mem), mem_kv[:, d_mem:].reshape(bp, n_mem, d_mem)
        wq, wo = ca_wq[l].astype(BF16), ca_wo[l].astype(BF16)
        g2, b2 = ln2_g[l].reshape(1, -1), ln2_b[l].reshape(1, -1)
        x_p = mem_attn_ln(x_p.reshape(bp, tp, d), wq, mem_k, mem_v, wo, g2, b2, ROW_TILE)
        x_s = mem_attn_ln(x_s.reshape(bs, ts, d), wq, cache_mem_k[l].reshape(bs, -1, d_mem),
                          cache_mem_v[l].reshape(bs, -1, d_mem), wo, g2, b2, ts)

        x = jnp.concatenate([x_p.reshape(n_p, d), x_s.reshape(n_s, d)], axis=0)
        m = moe_ffn(x, l, router_w[l], router_b[l].reshape(1, -1), wg, bg, wu, bu, wd, bd)
        x = residual_ln(x, m, ln3_g[l].reshape(1, -1), ln3_b[l].reshape(1, -1))
        x_p, x_s = x[:n_p].reshape(bp, tp, d), x[n_p:]

        p_states.append(st_fm)
        mem_states.append((mem_k.reshape(bp, n_mem, MEM_HEADS, MEM_HEAD_DIM), mem_v.reshape(bp, n_mem, MEM_HEADS, MEM_HEAD_DIM)))
        s_states.append([a.reshape(bs, ts, N_KV, HEAD_DIM) for a in (kc_s, vc_s, ks_s, vs_s, km_s, vm_s)])
        keep_s = min(WINDOW, wbuf + ts)
        s_win.append((kwin_t[:, :, -keep_s:], vwin_t[:, :, -keep_s:]))

    keep = min(WINDOW, tp)
    st = jnp.stack(p_states)
    p_out = [_token_minor(st[:, :, i * KV_W:(i + 1) * KV_W, :]) for i in range(6)]
    p_out += [_token_minor(st[:, :, i * KV_W:(i + 1) * KV_W, tp - keep:]) for i in (6, 7)]
    p_out += [jnp.stack([ms[i] for ms in mem_states]) for i in range(2)]
    s_out = [jnp.stack([ss[i] for ss in s_states]) for i in range(6)]
    s_out += [_token_minor(jnp.stack([w[i] for w in s_win])) for i in range(2)]
    return (x_p, x_s.reshape(bs, ts, d)) + tuple(p_out) + tuple(s_out)
```
